```python
import jax, jax.numpy as jnp
from jax import lax
import numpy as np

D_MODEL = 2048
BATCH = 16
SEQ = 256
DEPTH = 4
DEC_BATCH = 4
DEC_SEQ = 4096
PAST_LEN = 512

GRID_W = 64
N_MIXERS = 3
EXPAND = 2
D_INNER = EXPAND * D_MODEL
HEAD_DIM = 128
N_HEADS = D_INNER // HEAD_DIM
WIN_H_MAX = 8
WIN_W = 16
QBLK_W = 16
KBLK_W = 32
CONFORMER_K = 31
SHORTCONV_K = 3
ATTN_QBLOCK = 128
ATTN_SCALE = HEAD_DIM ** -0.5
EPS = 1e-6
NEG_INF = -1e30
N_LAYERS_A = (DEPTH + 2) // 3
N_LAYERS_B = (DEPTH + 1) // 3
N_LAYERS_C = DEPTH // 3

kernel_name = "hybrid_dit_conformer_natten_shortconv_step"


def rms_norm(x, g):
    xf = x.astype(jnp.float32)
    y = xf * lax.rsqrt(jnp.mean(xf * xf, axis=-1, keepdims=True) + EPS)
    return (y * g.astype(jnp.float32)).astype(x.dtype)


def layer_norm(x, g, b):
    xf = x.astype(jnp.float32)
    mu = jnp.mean(xf, axis=-1, keepdims=True)
    xc = xf - mu
    var = jnp.mean(xc * xc, axis=-1, keepdims=True)
    return (xc * lax.rsqrt(var + EPS) * g.astype(jnp.float32) + b.astype(jnp.float32)).astype(x.dtype)


def depthwise_conv(x, w):
    k = w.shape[0]
    return lax.conv_general_dilated(
        x, w[:, None, :].astype(x.dtype), window_strides=(1,),
        padding=[(k // 2, k // 2)], dimension_numbers=('NWC', 'WIO', 'NWC'),
        feature_group_count=x.shape[-1])


def modulation(cond, w_ada, b_ada):
    m = jax.nn.silu(cond) @ w_ada + b_ada
    return jnp.split(m, 3, axis=-1)


def adaln_pre(x, g_pre, shift, scale):
    return rms_norm(x, g_pre) * (1.0 + scale) + shift


def post_residual(x, m, g_post, gate):
    return x + gate * rms_norm(m, g_post)


def conformer_conv_mixer(h, w_in, conv_w, conv_b, ln_g, ln_b, w_out):
    a, b, z = jnp.split(h @ w_in, 3, axis=-1)
    u = a * jax.nn.sigmoid(b)
    u = depthwise_conv(u, conv_w) + conv_b
    u = jax.nn.silu(layer_norm(u, ln_g, ln_b))
    return (u * jax.nn.silu(z)) @ w_out


def short_conv_mixer(h, w_in, conv_w, w_out):
    bg, cg, v, z = jnp.split(h @ w_in, 4, axis=-1)
    y = bg * depthwise_conv(cg * v, conv_w)
    return (y * jax.nn.silu(z)) @ w_out


def attn_proj(h, w_in):
    q, k, v, z = jnp.split(h @ w_in, 4, axis=-1)
    b, l = h.shape[0], h.shape[1]
    heads = lambda t: t.reshape(b, l, N_HEADS, HEAD_DIM)
    return heads(q), heads(k), heads(v), z


def context_attention(q, k, v):
    b, l = q.shape[0], q.shape[1]
    nb = l // ATTN_QBLOCK
    qb = q.reshape(b, nb, ATTN_QBLOCK, N_HEADS, HEAD_DIM).transpose(1, 0, 2, 3, 4)

    def one_block(q_blk):
        s = jnp.einsum('bqhd,bkhd->bhqk', q_blk, k, preferred_element_type=jnp.float32) * ATTN_SCALE
        p = jax.nn.softmax(s, axis=-1).astype(v.dtype)
        return jnp.einsum('bhqk,bkhd->bqhd', p, v)

    o = lax.map(one_block, qb)
    return o.transpose(1, 0, 2, 3, 4).reshape(b, l, N_HEADS * HEAD_DIM)


def neighbourhood_attention(q, k, v, k_ctx, v_ctx, rpb):
    b, l = q.shape[0], q.shape[1]
    rows = l // GRID_W
    wh = min(WIN_H_MAX, rows)
    ncb = GRID_W // QBLK_W
    n_loc = wh * KBLK_W
    r_idx = np.arange(rows)
    r0 = np.clip(r_idx - wh // 2, 0, rows - wh)
    drow = r0[:, None] + np.arange(wh)[None, :] - r_idx[:, None] + (WIN_H_MAX - 1)
    qcol = np.arange(GRID_W).reshape(ncb, QBLK_W)
    kstart = np.clip(np.arange(ncb) * QBLK_W - WIN_W // 2, 0, GRID_W - KBLK_W)
    kcol = kstart[:, None] + np.arange(KBLK_W)[None, :]
    cstart = np.clip(qcol - WIN_W // 2, 0, GRID_W - WIN_W)
    col_ok = (kcol[:, None, :] >= cstart[:, :, None]) & (kcol[:, None, :] < cstart[:, :, None] + WIN_W)
    dcol = np.clip(kcol[:, None, :] - qcol[:, :, None] + (WIN_W - 1), 0, 2 * WIN_W - 2)
    mask = jnp.asarray(np.broadcast_to(col_ok[:, :, None, :], (ncb, QBLK_W, wh, KBLK_W)).reshape(ncb, QBLK_W, n_loc))
    dcol_j = jnp.asarray(dcol[:, :, None, :].astype(np.int32))

    kg = k.reshape(b, rows, GRID_W, N_HEADS, HEAD_DIM)
    vg = v.reshape(b, rows, GRID_W, N_HEADS, HEAD_DIM)
    qg = q.reshape(b, rows, ncb, QBLK_W, N_HEADS, HEAD_DIM).transpose(1, 0, 2, 3, 4, 5)

    def row_block(args):
        q_r, r0_r, drow_r = args
        k_r = lax.dynamic_slice_in_dim(kg, r0_r, wh, axis=1)[:, :, kcol]
        v_r = lax.dynamic_slice_in_dim(vg, r0_r, wh, axis=1)[:, :, kcol]
        k_r = k_r.transpose(0, 2, 1, 3, 4, 5).reshape(b, ncb, n_loc, N_HEADS, HEAD_DIM)
        v_r = v_r.transpose(0, 2, 1, 3, 4, 5).reshape(b, ncb, n_loc, N_HEADS, HEAD_DIM)
        bias = rpb[:, drow_r[None, None, :, None], dcol_j].reshape(N_HEADS, ncb, QBLK_W, n_loc)
        s_loc = jnp.einsum('bcqhd,bckhd->bhcqk', q_r, k_r, preferred_element_type=jnp.float32) * ATTN_SCALE
        s_loc = jnp.where(mask, s_loc + bias.astype(jnp.float32), NEG_INF)
        s_ctx = jnp.einsum('bcqhd,bkhd->bhcqk', q_r, k_ctx, preferred_element_type=jnp.float32) * ATTN_SCALE
        p = jax.nn.softmax(jnp.concatenate([s_loc, s_ctx], axis=-1), axis=-1).astype(v.dtype)
        return (jnp.einsum('bhcqk,bckhd->bcqhd', p[..., :n_loc], v_r)
                + jnp.einsum('bhcqk,bkhd->bcqhd', p[..., n_loc:], v_ctx))

    o = lax.map(row_block, (qg, jnp.asarray(r0, jnp.int32), jnp.asarray(drow, jnp.int32)))
    return o.transpose(1, 0, 2, 3, 4, 5).reshape(b, l, N_HEADS * HEAD_DIM)


def na_mixer_context(h, w_in, w_out):
    q, k, v, z = attn_proj(h, w_in)
    o = context_attention(q, k, v)
    return (o * jax.nn.silu(z)) @ w_out, k, v


def na_mixer_latent(h, w_in, rpb, w_out, k_ctx, v_ctx):
    q, k, v, z = attn_proj(h, w_in)
    o = neighbourhood_attention(q, k, v, k_ctx, v_ctx, rpb)
    return (o * jax.nn.silu(z)) @ w_out


def setup_inputs(seed: int = 0) -> dict:
    key = jax.random.key(seed)
    ks = jax.random.split(key, 20)
    nrm = lambda k, shape, s: jax.random.normal(k, shape, jnp.float32) * s
    D, E = D_MODEL, D_INNER
    return {
        "x_prompt": nrm(ks[0], (BATCH, SEQ, D), 1.0),
        "x_sample": nrm(ks[1], (DEC_BATCH, DEC_SEQ, D), 1.0),
        "cache_k": nrm(ks[2], (DEC_BATCH, N_LAYERS_B, PAST_LEN, N_HEADS, HEAD_DIM), 1.0),
        "cache_v": nrm(ks[3], (DEC_BATCH, N_LAYERS_B, PAST_LEN, N_HEADS, HEAD_DIM), 1.0),
        "c": nrm(ks[4], (DEC_BATCH, D), 1.0),
        "c_ctx": nrm(ks[5], (D,), 1.0),
        "w_ada": nrm(ks[6], (DEPTH, D, 3 * D), 0.5 * D ** -0.5),
        "b_ada": nrm(ks[7], (DEPTH, 3 * D), 0.02),
        "g_pre": 1.0 + nrm(ks[8], (DEPTH, D), 0.05),
        "g_post": 1.0 + nrm(ks[9], (DEPTH, D), 0.05),
        "w_in_a": nrm(ks[10], (N_LAYERS_A, D, 3 * E), D ** -0.5),
        "conv_w_a": nrm(ks[11], (N_LAYERS_A, CONFORMER_K, E), CONFORMER_K ** -0.5),
        "conv_b_a": nrm(ks[12], (N_LAYERS_A, E), 0.02),
        "ln_g_a": 1.0 + nrm(ks[13], (N_LAYERS_A, E), 0.05),
        "ln_b_a": nrm(ks[14], (N_LAYERS_A, E), 0.02),
        "w_in_b": nrm(ks[15], (N_LAYERS_B, D, 4 * E), D ** -0.5),
        "rpb_b": nrm(ks[16], (N_LAYERS_B, N_HEADS, 2 * WIN_H_MAX - 1, 2 * WIN_W - 1), 0.5),
        "w_in_c": nrm(ks[17], (N_LAYERS_C, D, 4 * E), D ** -0.5),
        "conv_w_c": nrm(ks[18], (N_LAYERS_C, SHORTCONV_K, E), SHORTCONV_K ** -0.5),
        "w_out": nrm(ks[19], (DEPTH, E, D), E ** -0.5),
    }


def reference(x_prompt, x_sample, cache_k, cache_v, c, c_ctx, w_ada, b_ada, g_pre, g_post,
              w_in_a, conv_w_a, conv_b_a, ln_g_a, ln_b_a, w_in_b, rpb_b, w_in_c, conv_w_c, w_out):
    cond_p = c_ctx[None, None, :]
    cond_s = c[:, None, :]
    xp, xs = x_prompt, x_sample
    ks_list, vs_list = [], []
    for i in range(DEPTH):
        kind, j = i % N_MIXERS, i // N_MIXERS
        sh_p, sc_p, gt_p = modulation(cond_p, w_ada[i], b_ada[i])
        sh_s, sc_s, gt_s = modulation(cond_s, w_ada[i], b_ada[i])
        hp = adaln_pre(xp, g_pre[i], sh_p, sc_p)
        hs = adaln_pre(xs, g_pre[i], sh_s, sc_s)
        if kind == 0:
            mp = conformer_conv_mixer(hp, w_in_a[j], conv_w_a[j], conv_b_a[j], ln_g_a[j], ln_b_a[j], w_out[i])
            ms = conformer_conv_mixer(hs, w_in_a[j], conv_w_a[j], conv_b_a[j], ln_g_a[j], ln_b_a[j], w_out[i])
        elif kind == 1:
            mp, kp, vp = na_mixer_context(hp, w_in_b[j], w_out[i])
            ks_list.append(kp)
            vs_list.append(vp)
            ms = na_mixer_latent(hs, w_in_b[j], rpb_b[j], w_out[i], cache_k[:, j], cache_v[:, j])
        else:
            mp = short_conv_mixer(hp, w_in_c[j], conv_w_c[j], w_out[i])
            ms = short_conv_mixer(hs, w_in_c[j], conv_w_c[j], w_out[i])
        xp = post_residual(xp, mp, g_post[i], gt_p)
        xs = post_residual(xs, ms, g_post[i], gt_s)
    new_k = jnp.stack(ks_list, axis=1)
    new_v = jnp.stack(vs_list, axis=1)
    return (xp, xs, new_k, new_v)
```

```python
import functools

import numpy as np
import jax
import jax.numpy as jnp
from jax import lax
from jax.experimental import pallas as pl
from jax.experimental.pallas import tpu as pltpu

F32 = jnp.float32
BF16 = jnp.bfloat16

GRID_W = 64
HEAD_DIM = 128
WIN_H = 8
WIN_W = 16
N_MIXERS = 3
EPS = 1e-6
NEG_INF = -1e30
ATTN_SCALE = HEAD_DIM ** -0.5

LANES = 128
HALO = 16
MOD_ROWS = 8
VMEM_LIMIT = 56 * 1024 * 1024
Q_ROWS = 8
K_ROWS = 16


def _params(n_axes):
    return pltpu.CompilerParams(dimension_semantics=("arbitrary",) * n_axes,
                                vmem_limit_bytes=VMEM_LIMIT)


def _sigmoid(x):
    return 1.0 / (1.0 + jnp.exp(-x))


def _silu(x):
    return x * _sigmoid(x)


def _mod_kernel(cond_ref, w_ref, b_ref, o_ref):
    s = _silu(cond_ref[...])
    o_ref[...] = jnp.dot(s, w_ref[...], preferred_element_type=F32,
                         precision=lax.Precision.HIGHEST) + b_ref[...]


def _modulation(c, c_ctx, w_ada, b_ada):
    depth, d, d3 = w_ada.shape
    pad = MOD_ROWS - 1 - c.shape[0]
    cond = jnp.concatenate([c_ctx[None, :], c, jnp.zeros((pad, d), F32)], axis=0)
    tn = 1024
    mods = pl.pallas_call(
        _mod_kernel,
        grid=(depth, d3 // tn),
        in_specs=[pl.BlockSpec((MOD_ROWS, d), lambda l, n: (0, 0)),
                  pl.BlockSpec((None, d, tn), lambda l, n: (l, 0, n)),
                  pl.BlockSpec((None, 1, tn), lambda l, n: (l, 0, n))],
        out_specs=pl.BlockSpec((None, MOD_ROWS, tn), lambda l, n: (l, 0, n)),
        out_shape=jax.ShapeDtypeStruct((depth, MOD_ROWS, d3), F32),
        compiler_params=_params(2),
        name="modulation",
    )(cond, w_ada, b_ada.reshape(depth, 1, d3))
    return mods.reshape(depth * MOD_ROWS, 1, d3)


def _mod_specs(layer, d, row_fn):
    def spec(col):
        return pl.BlockSpec((None, 1, d), lambda i: (layer * MOD_ROWS + row_fn(i), 0, col))
    return spec(0), spec(1), spec(2)


def _adaln(x, g, shift, scale):
    y = x * lax.rsqrt(jnp.mean(x * x, axis=-1, keepdims=True) + EPS)
    return (y * g) * (1.0 + scale) + shift


def _prenorm_kernel(x_ref, g_ref, sh_ref, sc_ref, h_ref):
    h_ref[...] = _adaln(x_ref[...], g_ref[...], sh_ref[...], sc_ref[...]).astype(BF16)


def _prenorm(x, mods, g_pre, layer, row_fn, tm):
    t, d = x.shape
    sh, sc, _ = _mod_specs(layer, d, row_fn)
    return pl.pallas_call(
        _prenorm_kernel,
        grid=(t // tm,),
        in_specs=[pl.BlockSpec((tm, d), lambda i: (i, 0)),
                  pl.BlockSpec((None, 1, d), lambda i: (layer, 0, 0)),
                  sh, sc],
        out_specs=pl.BlockSpec((tm, d), lambda i: (i, 0)),
        out_shape=jax.ShapeDtypeStruct((t, d), BF16),
        compiler_params=_params(1),
        name="prenorm",
    )(x, g_pre.reshape(g_pre.shape[0], 1, d), mods, mods)


def _inproj_kernel(kind, n_groups, tn, emit_kv, h_ref, *refs):
    w_refs = refs[:n_groups]
    out_refs = refs[n_groups:-1]
    wbf_ref = refs[-1]

    @pl.when(pl.program_id(1) == 0)
    def _cast_weights():
        for g in range(n_groups):
            wbf_ref[:, g * tn:(g + 1) * tn] = w_refs[g][...].astype(BF16)

    r = jnp.dot(h_ref[...], wbf_ref[...], preferred_element_type=F32)
    p = [r[:, g * tn:(g + 1) * tn] for g in range(n_groups)]
    if kind == 0:
        a, b, z = p
        out_refs[0][...] = (a * _sigmoid(b)).astype(BF16)
        out_refs[1][...] = _silu(z).astype(BF16)
    elif kind == 1:
        q, k, v, z = p
        out_refs[0][...] = (q * ATTN_SCALE).astype(BF16)
        out_refs[1][...] = k.astype(BF16)
        out_refs[2][...] = v.astype(BF16)
        out_refs[3][...] = _silu(z).astype(BF16)
        if emit_kv:
            out_refs[4][...] = k
            out_refs[5][...] = v
    else:
        bg, cg, v, z = p
        out_refs[0][...] = (cg * v).astype(BF16)
        out_refs[1][...] = (bg * _silu(z)).astype(BF16)


def _inproj(h, w_in, j, kind, emit_kv=False):
    t, d = h.shape
    n_groups = 3 if kind == 0 else 4
    e = w_in.shape[-1] // n_groups
    tm = min(1024, t)
    tn = 256
    nj = e // tn
    w_specs = [pl.BlockSpec((None, d, tn), functools.partial(lambda jj, ii, g: (j, 0, g * nj + jj), g=g))
               for g in range(n_groups)]
    n_bf = {0: 2, 1: 4, 2: 2}[kind]
    out_shape = [jax.ShapeDtypeStruct((t, e), BF16)] * n_bf
    if emit_kv:
        out_shape = out_shape + [jax.ShapeDtypeStruct((t, e), F32)] * 2
    out_spec = pl.BlockSpec((tm, tn), lambda jj, ii: (ii, jj))
    return pl.pallas_call(
        functools.partial(_inproj_kernel, kind, n_groups, tn, emit_kv),
        grid=(nj, t // tm),
        in_specs=[pl.BlockSpec((tm, d), lambda jj, ii: (ii, 0))] + w_specs,
        out_specs=[out_spec] * len(out_shape),
        out_shape=out_shape,
        scratch_shapes=[pltpu.VMEM((d, n_groups * tn), BF16)],
        compiler_params=_params(2),
        name="inproj",
    )(h, *([w_in] * n_groups))


def _outproj_kernel(has_next, g_ref, w_ref, x_ref, gpost_ref, gate_ref, *refs):
    m = jnp.dot(g_ref[...], w_ref[...], preferred_element_type=F32)
    y = m * lax.rsqrt(jnp.mean(m * m, axis=-1, keepdims=True) + EPS)
    xn = x_ref[...] + gate_ref[...] * (y * gpost_ref[...])
    if has_next:
        gpre_ref, sh_ref, sc_ref, xo_ref, h_ref = refs
        h_ref[...] = _adaln(xn, gpre_ref[...], sh_ref[...], sc_ref[...]).astype(BF16)
    else:
        (xo_ref,) = refs
    xo_ref[...] = xn


def _outproj(g, w_out_bf, layer, x, mods, g_post, g_pre, row_fn, tm):
    t, e = g.shape
    d = x.shape[1]
    depth = w_out_bf.shape[0]
    has_next = layer + 1 < depth
    _, _, gate = _mod_specs(layer, d, row_fn)
    row = lambda i: (i, 0)
    in_specs = [pl.BlockSpec((tm, e), row),
                pl.BlockSpec((None, e, d), lambda i: (layer, 0, 0), pipeline_mode=pl.Buffered(1)),
                pl.BlockSpec((tm, d), row),
                pl.BlockSpec((None, 1, d), lambda i: (layer, 0, 0)),
                gate]
    args = [g, w_out_bf, x, g_post.reshape(depth, 1, d), mods]
    out_specs = [pl.BlockSpec((tm, d), row)]
    out_shape = [jax.ShapeDtypeStruct((t, d), F32)]
    if has_next:
        sh, sc, _ = _mod_specs(layer + 1, d, row_fn)
        in_specs += [pl.BlockSpec((None, 1, d), lambda i: (layer + 1, 0, 0)), sh, sc]
        args += [g_pre.reshape(depth, 1, d), mods, mods]
        out_specs.append(pl.BlockSpec((tm, d), row))
        out_shape.append(jax.ShapeDtypeStruct((t, d), BF16))
    outs = pl.pallas_call(
        functools.partial(_outproj_kernel, has_next),
        grid=(t // tm,),
        in_specs=in_specs,
        out_specs=out_specs,
        out_shape=out_shape,
        compiler_params=_params(1),
        name="outproj",
    )(*args)
    return (outs[0], outs[1]) if has_next else (outs[0], None)


def _conv_kernel(taps, conformer, ts, n_col, row_chunk, u_ref, up_ref, un_ref, aux_ref, w_ref, *refs):
    if conformer:
        cb_ref, lg_ref, lb_ref, o_ref, ubuf, cbuf = refs
    else:
        o_ref, ubuf, cbuf = refs
    s = pl.program_id(1)
    has_prev = s > 0
    has_next = s < pl.num_programs(1) - 1
    pad = taps // 2
    for c in range(n_col):
        sl = slice(c * LANES, (c + 1) * LANES)
        ubuf[c, 0:HALO, :] = jnp.where(has_prev, up_ref[:, sl].astype(F32), 0.0)
        ubuf[c, HALO:HALO + ts, :] = u_ref[:, sl].astype(F32)
        ubuf[c, HALO + ts:2 * HALO + ts, :] = jnp.where(has_next, un_ref[:, sl].astype(F32), 0.0)

    def col_body(c, carry):
        for r0 in range(0, ts, row_chunk):
            acc = jnp.zeros((row_chunk, LANES), F32)
            for k in range(taps):
                acc = acc + ubuf[c, pl.ds(HALO - pad + k + r0, row_chunk), :] * w_ref[c, k:k + 1, :]
            if conformer:
                acc = acc + cb_ref[c]
            cbuf[c, r0:r0 + row_chunk, :] = acc
        return carry

    lax.fori_loop(0, n_col, col_body, 0)

    if conformer:
        e = n_col * LANES
        tot = cbuf[0]
        for c in range(1, n_col):
            tot = tot + cbuf[c]
        mu = jnp.sum(tot, axis=-1, keepdims=True) / e
        d0 = cbuf[0] - mu
        tot = d0 * d0
        for c in range(1, n_col):
            dc = cbuf[c] - mu
            tot = tot + dc * dc
        rstd = lax.rsqrt(jnp.sum(tot, axis=-1, keepdims=True) / e + EPS)
    for c in range(n_col):
        sl = slice(c * LANES, (c + 1) * LANES)
        y = cbuf[c]
        if conformer:
            y = _silu((y - mu) * rstd * lg_ref[c] + lb_ref[c])
        o_ref[:, sl] = (y * aux_ref[:, sl].astype(F32)).astype(BF16)


def _cols(v, n_col):
    lead = v.shape[:-1]
    v = v.reshape(lead + (n_col, LANES))
    return jnp.moveaxis(v, -2, 0)


def _conv_mixer(u, aux, nb, seq, conv_w, conformer, conv_b=None, ln_g=None, ln_b=None):
    t, e = u.shape
    taps = conv_w.shape[0]
    n_col = e // LANES
    ts = 256
    ns = seq // ts
    hb = ts // HALO
    n_halo = seq // HALO
    u3 = u.reshape(nb, seq, e)
    aux3 = aux.reshape(nb, seq, e)
    tile = pl.BlockSpec((None, ts, e), lambda b, s: (b, s, 0))
    prev = pl.BlockSpec((None, HALO, e), lambda b, s: (b, jnp.maximum(s * hb - 1, 0), 0))
    nxt = pl.BlockSpec((None, HALO, e), lambda b, s: (b, jnp.minimum((s + 1) * hb, n_halo - 1), 0))
    in_specs = [tile, prev, nxt, tile,
                pl.BlockSpec((n_col, taps, LANES), lambda b, s: (0, 0, 0))]
    args = [u3, u3, u3, aux3, _cols(conv_w, n_col)]
    if conformer:
        vec = pl.BlockSpec((n_col, 1, LANES), lambda b, s: (0, 0, 0))
        in_specs += [vec, vec, vec]
        args += [_cols(conv_b[None, :], n_col), _cols(ln_g[None, :], n_col), _cols(ln_b[None, :], n_col)]
    out = pl.pallas_call(
        functools.partial(_conv_kernel, taps, conformer, ts, n_col, 128),
        grid=(nb, ns),
        in_specs=in_specs,
        out_specs=tile,
        out_shape=jax.ShapeDtypeStruct((nb, seq, e), BF16),
        scratch_shapes=[pltpu.VMEM((n_col, ts + 2 * HALO, LANES), F32),
                        pltpu.VMEM((n_col, ts, LANES), F32)],
        compiler_params=_params(2),
        name="conformer_conv" if conformer else "short_conv",
    )(*args)
    return out.reshape(t, e)


def _nt_dot(a, b):
    return lax.dot_general(a, b, (((1,), (1,)), ((), ())), preferred_element_type=F32)


def _ctx_attn_kernel(n_heads, q_ref, k_ref, v_ref, sz_ref, o_ref):
    for hh in range(n_heads):
        sl = slice(hh * HEAD_DIM, (hh + 1) * HEAD_DIM)
        s = _nt_dot(q_ref[:, sl], k_ref[:, sl])
        p = jnp.exp(s - jnp.max(s, axis=-1, keepdims=True))
        l = jnp.sum(p, axis=-1, keepdims=True)
        o = jnp.dot(p.astype(BF16), v_ref[:, sl], preferred_element_type=F32) / l
        o_ref[:, sl] = (o * sz_ref[:, sl].astype(F32)).astype(BF16)


def _ctx_attention(q, k, v, sz, nb, seq):
    t, e = q.shape
    heads_per_step = 8
    w = heads_per_step * HEAD_DIM
    blk = pl.BlockSpec((seq, w), lambda b, g: (b, g))
    return pl.pallas_call(
        functools.partial(_ctx_attn_kernel, heads_per_step),
        grid=(nb, e // w),
        in_specs=[blk] * 4,
        out_specs=blk,
        out_shape=jax.ShapeDtypeStruct((t, e), BF16),
        compiler_params=_params(2),
        name="ctx_attention",
    )(q, k, v, sz)


def _nbr_geometry(rows):
    n_blocks = rows // Q_ROWS
    classes = {}
    for name, m in (("first", 0), ("mid", 1), ("last", n_blocks - 1)):
        kr0 = int(np.clip(Q_ROWS * m - WIN_H // 2, 0, rows - K_ROWS))
        per_row = []
        for i in range(Q_ROWS):
            r = Q_ROWS * m + i
            r0 = int(np.clip(r - WIN_H // 2, 0, rows - WIN_H))
            a = r0 - kr0
            assert 0 <= a and a + WIN_H <= K_ROWS
            jp0, jp1 = a // 2, (a + WIN_H - 1) // 2
            pairs = []
            for jp in range(jp0, jp1 + 1):
                halves = []
                for j in (2 * jp, 2 * jp + 1):
                    ok = a <= j < a + WIN_H
                    halves.append(kr0 + j - r + WIN_H - 1 if ok else None)
                pairs.append(tuple(halves))
            per_row.append((jp0, pairs))
        classes[name] = (kr0 - Q_ROWS * m, per_row)
    return classes


def _nbr_kernel(geom, tab_index, n_bias_rows, n_bias_cols,
                rpb_ref, q_ref, k_ref, v_ref, sz_ref, kc_ref, vc_ref, o_ref,
                half_l, half_r, tabs, s_loc, s_ctx, p_loc, p_ctx, linv, kcb, vcb):
    h = pl.program_id(0)
    b = pl.program_id(1)
    n_blocks = q_ref.shape[0] // (Q_ROWS * GRID_W)
    nq = Q_ROWS * GRID_W
    nk = K_ROWS * GRID_W

    @pl.when(b == 0)
    def _build_bias_tables():
        qc = lax.broadcasted_iota(jnp.int32, (GRID_W, 2 * GRID_W), 0)
        lane = lax.broadcasted_iota(jnp.int32, (GRID_W, 2 * GRID_W), 1)
        kc = lane & (GRID_W - 1)
        right = lane >= GRID_W
        cstart = jnp.clip(qc - WIN_W // 2, 0, GRID_W - WIN_W)
        col_ok = (kc >= cstart) & (kc < cstart + WIN_W)
        dcol = kc - qc + (WIN_W - 1)
        code_l = jnp.where(col_ok & jnp.logical_not(right), dcol, -1)
        code_r = jnp.where(col_ok & right, dcol, -1)
        base = h * (n_bias_rows * n_bias_cols)
        for d in range(n_bias_rows):
            acc_l = jnp.full((GRID_W, 2 * GRID_W), NEG_INF, F32)
            acc_r = acc_l
            for dc in range(n_bias_cols):
                val = rpb_ref[base + d * n_bias_cols + dc]
                acc_l = jnp.where(code_l == dc, val, acc_l)
                acc_r = jnp.where(code_r == dc, val, acc_r)
            half_l[d] = acc_l
            half_r[d] = acc_r
        for (dl, dr), t in tab_index.items():
            if dl is not None and dr is not None:
                tabs[t] = jnp.maximum(half_l[dl], half_r[dr])
            elif dl is not None:
                tabs[t] = half_l[dl]
            else:
                tabs[t] = half_r[dr]

    kcb[...] = kc_ref[...].astype(BF16)
    vcb[...] = vc_ref[...].astype(BF16)

    def block(cls, q0):
        k_off, per_row = geom[cls]
        k0 = q0 + k_off * GRID_W
        if not isinstance(q0, int):
            q0 = pl.multiple_of(q0, nq)
            k0 = pl.multiple_of(k0, nq // 2)
        q = q_ref[pl.ds(q0, nq), :]
        s_loc[...] = _nt_dot(q, k_ref[pl.ds(k0, nk), :])
        s_ctx[...] = _nt_dot(q, kcb[...])
        for i in range(Q_ROWS):
            rs = slice(i * GRID_W, (i + 1) * GRID_W)
            jp0, pairs = per_row[i]
            lo, hi = jp0 * LANES, (jp0 + len(pairs)) * LANES
            sl = [s_loc[rs, lo + t * LANES:lo + (t + 1) * LANES] + tabs[tab_index[pr]]
                  for t, pr in enumerate(pairs)]
            sc = s_ctx[rs, :]
            mx = jnp.max(sc, axis=-1, keepdims=True)
            for x in sl:
                mx = jnp.maximum(mx, jnp.max(x, axis=-1, keepdims=True))
            pc = jnp.exp(sc - mx)
            l = jnp.sum(pc, axis=-1, keepdims=True)
            p_ctx[rs, :] = pc.astype(BF16)
            if lo > 0:
                p_loc[rs, 0:lo] = jnp.zeros((GRID_W, lo), BF16)
            if hi < nk:
                p_loc[rs, hi:nk] = jnp.zeros((GRID_W, nk - hi), BF16)
            for t, x in enumerate(sl):
                px = jnp.exp(x - mx)
                l = l + jnp.sum(px, axis=-1, keepdims=True)
                p_loc[rs, lo + t * LANES:lo + (t + 1) * LANES] = px.astype(BF16)
            linv[rs, :] = jnp.broadcast_to(1.0 / l, (GRID_W, HEAD_DIM))
        o = jnp.dot(p_loc[...], v_ref[pl.ds(k0, nk), :], preferred_element_type=F32)
        o = o + jnp.dot(p_ctx[...], vcb[...], preferred_element_type=F32)
        o = o * linv[...] * sz_ref[pl.ds(q0, nq), :].astype(F32)
        o_ref[pl.ds(q0, nq), :] = o.astype(BF16)

    block("first", 0)

    def mid_body(m, carry):
        block("mid", m * nq)
        return carry

    lax.fori_loop(1, n_blocks - 1, mid_body, 0)
    block("last", (n_blocks - 1) * nq)


def _nbr_attention(q, k, v, sz, cache_k, cache_v, j, rpb):
    t, e = q.shape
    nb, _, past, n_heads, hd = cache_k.shape
    seq = t // nb
    rows = seq // GRID_W
    assert hd == HEAD_DIM and e == n_heads * HEAD_DIM and seq == rows * GRID_W
    assert rows % Q_ROWS == 0 and rows >= 3 * Q_ROWS and GRID_W * 2 == LANES
    n_bias_rows, n_bias_cols = rpb.shape[1], rpb.shape[2]
    assert n_bias_rows == 2 * WIN_H - 1 and n_bias_cols == 2 * WIN_W - 1
    geom = _nbr_geometry(rows)
    keys = sorted({pr for _, per_row in geom.values() for _, pairs in per_row for pr in pairs},
                  key=lambda pr: (pr[0] is None, pr[1] is None, pr))
    tab_index = {pr: n for n, pr in enumerate(keys)}
    nq, nk = Q_ROWS * GRID_W, K_ROWS * GRID_W
    head = pl.BlockSpec((seq, HEAD_DIM), lambda h, b: (b, h))
    ctx = pl.BlockSpec((None, None, past, HEAD_DIM), lambda h, b: (b, j, 0, h))
    ck = cache_k.reshape(nb, cache_k.shape[1], past, e)
    cv = cache_v.reshape(nb, cache_v.shape[1], past, e)
    return pl.pallas_call(
        functools.partial(_nbr_kernel, geom, tab_index, n_bias_rows, n_bias_cols),
        grid=(n_heads, nb),
        in_specs=[pl.BlockSpec(memory_space=pltpu.SMEM), head, head, head, head, ctx, ctx],
        out_specs=head,
        out_shape=jax.ShapeDtypeStruct((t, e), BF16),
        scratch_shapes=[pltpu.VMEM((n_bias_rows, GRID_W, LANES), F32),
                        pltpu.VMEM((n_bias_rows, GRID_W, LANES), F32),
                        pltpu.VMEM((len(keys), GRID_W, LANES), F32),
                        pltpu.VMEM((nq, nk), F32),
                        pltpu.VMEM((nq, past), F32),
                        pltpu.VMEM((nq, nk), BF16),
                        pltpu.VMEM((nq, past), BF16),
                        pltpu.VMEM((nq, HEAD_DIM), F32),
                        pltpu.VMEM((past, HEAD_DIM), BF16),
                        pltpu.VMEM((past, HEAD_DIM), BF16)],
        compiler_params=_params(2),
        name="nbr_attention",
    )(rpb.reshape(-1), q, k, v, sz, ck, cv)


def kernel(x_prompt, x_sample, cache_k, cache_v, c, c_ctx, w_ada, b_ada, g_pre, g_post,
           w_in_a, conv_w_a, conv_b_a, ln_g_a, ln_b_a, w_in_b, rpb_b, w_in_c, conv_w_c, w_out):
    depth = w_ada.shape[0]
    d = x_prompt.shape[-1]
    tm = 512
    mods = _modulation(c, c_ctx, w_ada, b_ada)
    w_out_bf = w_out.astype(BF16)

    groups = []
    for x, is_ctx in ((x_prompt, True), (x_sample, False)):
        nb, seq, _ = x.shape
        if is_ctx:
            row_fn = lambda i: 0
        else:
            row_fn = functools.partial(lambda i, per: 1 + i // per, per=seq // tm)
        x2 = x.reshape(nb * seq, d)
        groups.append(dict(x=x2, h=_prenorm(x2, mods, g_pre, 0, row_fn, tm),
                           nb=nb, seq=seq, row_fn=row_fn, is_ctx=is_ctx))

    new_k, new_v = [], []
    for i in range(depth):
        kind, j = i % N_MIXERS, i // N_MIXERS
        for grp in groups:
            nb, seq = grp["nb"], grp["seq"]
            if kind == 0:
                u, sz = _inproj(grp["h"], w_in_a, j, 0)
                g = _conv_mixer(u, sz, nb, seq, conv_w_a[j], True, conv_b_a[j], ln_g_a[j], ln_b_a[j])
            elif kind == 1:
                if grp["is_ctx"]:
                    q, k, v, sz, k32, v32 = _inproj(grp["h"], w_in_b, j, 1, emit_kv=True)
                    g = _ctx_attention(q, k, v, sz, nb, seq)
                    new_k.append(k32.reshape(nb, seq, -1, HEAD_DIM))
                    new_v.append(v32.reshape(nb, seq, -1, HEAD_DIM))
                else:
                    q, k, v, sz = _inproj(grp["h"], w_in_b, j, 1)
                    g = _nbr_attention(q, k, v, sz, cache_k, cache_v, j, rpb_b[j])
            else:
                cv, bz = _inproj(grp["h"], w_in_c, j, 2)
                g = _conv_mixer(cv, bz, nb, seq, conv_w_c[j], False)
            grp["x"], grp["h"] = _outproj(g, w_out_bf, i, grp["x"], mods, g_post, g_pre, grp["row_fn"], tm)

    y_prompt = groups[0]["x"].reshape(x_prompt.shape)
    y_sample = groups[1]["x"].reshape(x_sample.shape)
    return (y_prompt, y_sample, jnp.stack(new_k, axis=1), jnp.stack(new_v, axis=1))
```

```python
import functools

import numpy as np
import jax
import jax.numpy as jnp
from jax import lax
from jax.experimental import pallas as pl
from jax.experimental.pallas import tpu as pltpu

F32 = jnp.float32
BF16 = jnp.bfloat16

GRID_W = 64
HEAD_DIM = 128
WIN_H = 8
WIN_W = 16
N_MIXERS = 3
EPS = 1e-6
NEG_INF = -1e30
ATTN_SCALE = HEAD_DIM ** -0.5

LANES = 128
HALO = 16
MOD_ROWS = 8
VMEM_LIMIT = 56 * 1024 * 1024
Q_ROWS = 8
K_ROWS = 16


def _params(n_axes):
    return pltpu.CompilerParams(dimension_semantics=("arbitrary",) * n_axes,
                                vmem_limit_bytes=VMEM_LIMIT)


def _sigmoid(x):
    return 1.0 / (1.0 + jnp.exp(-x))


def _silu(x):
    return x * _sigmoid(x)


def _mod_kernel(cond_ref, w_ref, b_ref, o_ref):
    s = _silu(cond_ref[...])
    o_ref[...] = jnp.dot(s, w_ref[...], preferred_element_type=F32,
                         precision=lax.Precision.HIGHEST) + b_ref[...]


def _modulation(c, c_ctx, w_ada, b_ada):
    depth, d, d3 = w_ada.shape
    pad = MOD_ROWS - 1 - c.shape[0]
    cond = jnp.concatenate([c_ctx[None, :], c, jnp.zeros((pad, d), F32)], axis=0)
    tn = 1024
    mods = pl.pallas_call(
        _mod_kernel,
        grid=(depth, d3 // tn),
        in_specs=[pl.BlockSpec((MOD_ROWS, d), lambda l, n: (0, 0)),
                  pl.BlockSpec((None, d, tn), lambda l, n: (l, 0, n)),
                  pl.BlockSpec((None, 1, tn), lambda l, n: (l, 0, n))],
        out_specs=pl.BlockSpec((None, MOD_ROWS, tn), lambda l, n: (l, 0, n)),
        out_shape=jax.ShapeDtypeStruct((depth, MOD_ROWS, d3), F32),
        compiler_params=_params(2),
        name="modulation",
    )(cond, w_ada, b_ada.reshape(depth, 1, d3))
    return mods.reshape(depth * MOD_ROWS, 1, d3)


def _mod_specs(layer, d, row_fn):
    def spec(col):
        return pl.BlockSpec((None, 1, d), lambda i: (layer * MOD_ROWS + row_fn(i), 0, col))
    return spec(0), spec(1), spec(2)


def _adaln(x, g, shift, scale):
    y = x * lax.rsqrt(jnp.mean(x * x, axis=-1, keepdims=True) + EPS)
    return (y * g) * (1.0 + scale) + shift


def _prenorm_kernel(x_ref, g_ref, sh_ref, sc_ref, h_ref):
    h_ref[...] = _adaln(x_ref[...], g_ref[...], sh_ref[...], sc_ref[...]).astype(BF16)


def _prenorm(x, mods, g_pre, layer, row_fn, tm):
    t, d = x.shape
    sh, sc, _ = _mod_specs(layer, d, row_fn)
    return pl.pallas_call(
        _prenorm_kernel,
        grid=(t // tm,),
        in_specs=[pl.BlockSpec((tm, d), lambda i: (i, 0)),
                  pl.BlockSpec((None, 1, d), lambda i: (layer, 0, 0)),
                  sh, sc],
        out_specs=pl.BlockSpec((tm, d), lambda i: (i, 0)),
        out_shape=jax.ShapeDtypeStruct((t, d), BF16),
        compiler_params=_params(1),
        name="prenorm",
    )(x, g_pre.reshape(g_pre.shape[0], 1, d), mods, mods)


def _inproj_kernel(kind, n_groups, tn, emit_kv, h_ref, *refs):
    w_refs = refs[:n_groups]
    out_refs = refs[n_groups:-1]
    wbf_ref = refs[-1]

    @pl.when(pl.program_id(1) == 0)
    def _cast_weights():
        for g in range(n_groups):
            wbf_ref[:, g * tn:(g + 1) * tn] = w_refs[g][...].astype(BF16)

    r = jnp.dot(h_ref[...], wbf_ref[...], preferred_element_type=F32)
    p = [r[:, g * tn:(g + 1) * tn] for g in range(n_groups)]
    if kind == 0:
        a, b, z = p
        out_refs[0][...] = (a * _sigmoid(b)).astype(BF16)
        out_refs[1][...] = _silu(z).astype(BF16)
    elif kind == 1:
        q, k, v, z = p
        out_refs[0][...] = (q * ATTN_SCALE).astype(BF16)
        out_refs[1][...] = k.astype(BF16)
        out_refs[2][...] = v.astype(BF16)
        out_refs[3][...] = _silu(z).astype(BF16)
        if emit_kv:
            out_refs[4][...] = k
            out_refs[5][...] = v
    else:
        bg, cg, v, z = p
        out_refs[0][...] = (cg * v).astype(BF16)
        out_refs[1][...] = (bg * _silu(z)).astype(BF16)


def _inproj(h, w_in, j, kind, emit_kv=False):
    t, d = h.shape
    n_groups = 3 if kind == 0 else 4
    e = w_in.shape[-1] // n_groups
    tm = min(1024, t)
    tn = 256
    nj = e // tn
    w_specs = [pl.BlockSpec((None, d, tn), functools.partial(lambda jj, ii, g: (j, 0, g * nj + jj), g=g))
               for g in range(n_groups)]
    n_bf = {0: 2, 1: 4, 2: 2}[kind]
    out_shape = [jax.ShapeDtypeStruct((t, e), BF16)] * n_bf
    if emit_kv:
        out_shape = out_shape + [jax.ShapeDtypeStruct((t, e), F32)] * 2
    out_spec = pl.BlockSpec((tm, tn), lambda jj, ii: (ii, jj))
    return pl.pallas_call(
        functools.partial(_inproj_kernel, kind, n_groups, tn, emit_kv),
        grid=(nj, t // tm),
        in_specs=[pl.BlockSpec((tm, d), lambda jj, ii: (ii, 0))] + w_specs,
        out_specs=[out_spec] * len(out_shape),
        out_shape=out_shape,
        scratch_shapes=[pltpu.VMEM((d, n_groups * tn), BF16)],
        compiler_params=_params(2),
        name="inproj",
    )(h, *([w_in] * n_groups))


def _outproj_kernel(has_next, g_ref, w_ref, x_ref, gpost_ref, gate_ref, *refs):
    m = jnp.dot(g_ref[...], w_ref[...], preferred_element_type=F32)
    y = m * lax.rsqrt(jnp.mean(m * m, axis=-1, keepdims=True) + EPS)
    xn = x_ref[...] + gate_ref[...] * (y * gpost_ref[...])
    if has_next:
        gpre_ref, sh_ref, sc_ref, xo_ref, h_ref = refs
        h_ref[...] = _adaln(xn, gpre_ref[...], sh_ref[...], sc_ref[...]).astype(BF16)
    else:
        (xo_ref,) = refs
    xo_ref[...] = xn


def _outproj(g, w_out_bf, layer, x, mods, g_post, g_pre, row_fn, tm):
    t, e = g.shape
    d = x.shape[1]
    depth = w_out_bf.shape[0]
    has_next = layer + 1 < depth
    _, _, gate = _mod_specs(layer, d, row_fn)
    row = lambda i: (i, 0)
    in_specs = [pl.BlockSpec((tm, e), row),
                pl.BlockSpec((None, e, d), lambda i: (layer, 0, 0), pipeline_mode=pl.Buffered(1)),
                pl.BlockSpec((tm, d), row),
                pl.BlockSpec((None, 1, d), lambda i: (layer, 0, 0)),
                gate]
    args = [g, w_out_bf, x, g_post.reshape(depth, 1, d), mods]
    out_specs = [pl.BlockSpec((tm, d), row)]
    out_shape = [jax.ShapeDtypeStruct((t, d), F32)]
    if has_next:
        sh, sc, _ = _mod_specs(layer + 1, d, row_fn)
        in_specs += [pl.BlockSpec((None, 1, d), lambda i: (layer + 1, 0, 0)), sh, sc]
        args += [g_pre.reshape(depth, 1, d), mods, mods]
        out_specs.append(pl.BlockSpec((tm, d), row))
        out_shape.append(jax.ShapeDtypeStruct((t, d), BF16))
    outs = pl.pallas_call(
        functools.partial(_outproj_kernel, has_next),
        grid=(t // tm,),
        in_specs=in_specs,
        out_specs=out_specs,
        out_shape=out_shape,
        compiler_params=_params(1),
        name="outproj",
    )(*args)
    return (outs[0], outs[1]) if has_next else (outs[0], None)


def _conv_kernel(taps, conformer, ts, n_col, row_chunk, u_ref, up_ref, un_ref, aux_ref, w_ref, *refs):
    if conformer:
        cb_ref, lg_ref, lb_ref, o_ref, ubuf, cbuf = refs
    else:
        o_ref, ubuf, cbuf = refs
    s = pl.program_id(1)
    has_prev = s > 0
    has_next = s < pl.num_programs(1) - 1
    pad = taps // 2
    for c in range(n_col):
        sl = slice(c * LANES, (c + 1) * LANES)
        ubuf[c, 0:HALO, :] = jnp.where(has_prev, up_ref[:, sl].astype(F32), 0.0)
        ubuf[c, HALO:HALO + ts, :] = u_ref[:, sl].astype(F32)
        ubuf[c, HALO + ts:2 * HALO + ts, :] = jnp.where(has_next, un_ref[:, sl].astype(F32), 0.0)

    def col_body(c, carry):
        for r0 in range(0, ts, row_chunk):
            acc = jnp.zeros((row_chunk, LANES), F32)
            for k in range(taps):
                acc = acc + ubuf[c, pl.ds(HALO - pad + k + r0, row_chunk), :] * w_ref[c, k:k + 1, :]
            if conformer:
                acc = acc + cb_ref[c]
            cbuf[c, r0:r0 + row_chunk, :] = acc
        return carry

    lax.fori_loop(0, n_col, col_body, 0)

    if conformer:
        e = n_col * LANES
        tot = cbuf[0]
        for c in range(1, n_col):
            tot = tot + cbuf[c]
        mu = jnp.sum(tot, axis=-1, keepdims=True) / e
        d0 = cbuf[0] - mu
        tot = d0 * d0
        for c in range(1, n_col):
            dc = cbuf[c] - mu
            tot = tot + dc * dc
        rstd = lax.rsqrt(jnp.sum(tot, axis=-1, keepdims=True) / e + EPS)
    for c in range(n_col):
        sl = slice(c * LANES, (c + 1) * LANES)
        y = cbuf[c]
        if conformer:
            y = _silu((y - mu) * rstd * lg_ref[c] + lb_ref[c])
        o_ref[:, sl] = (y * aux_ref[:, sl].astype(F32)).astype(BF16)


def _cols(v, n_col):
    lead = v.shape[:-1]
    v = v.reshape(lead + (n_col, LANES))
    return jnp.moveaxis(v, -2, 0)


def _conv_mixer(u, aux, nb, seq, conv_w, conformer, conv_b=None, ln_g=None, ln_b=None):
    t, e = u.shape
    taps = conv_w.shape[0]
    n_col = e // LANES
    ts = 256
    ns = seq // ts
    hb = ts // HALO
    n_halo = seq // HALO
    u3 = u.reshape(nb, seq, e)
    aux3 = aux.reshape(nb, seq, e)
    tile = pl.BlockSpec((None, ts, e), lambda b, s: (b, s, 0))
    prev = pl.BlockSpec((None, HALO, e), lambda b, s: (b, jnp.maximum(s * hb - 1, 0), 0))
    nxt = pl.BlockSpec((None, HALO, e), lambda b, s: (b, jnp.minimum((s + 1) * hb, n_halo - 1), 0))
    in_specs = [tile, prev, nxt, tile,
                pl.BlockSpec((n_col, taps, LANES), lambda b, s: (0, 0, 0))]
    args = [u3, u3, u3, aux3, _cols(conv_w, n_col)]
    if conformer:
        vec = pl.BlockSpec((n_col, 1, LANES), lambda b, s: (0, 0, 0))
        in_specs += [vec, vec, vec]
        args += [_cols(conv_b[None, :], n_col), _cols(ln_g[None, :], n_col), _cols(ln_b[None, :], n_col)]
    out = pl.pallas_call(
        functools.partial(_conv_kernel, taps, conformer, ts, n_col, 128),
        grid=(nb, ns),
        in_specs=in_specs,
        out_specs=tile,
        out_shape=jax.ShapeDtypeStruct((nb, seq, e), BF16),
        scratch_shapes=[pltpu.VMEM((n_col, ts + 2 * HALO, LANES), F32),
                        pltpu.VMEM((n_col, ts, LANES), F32)],
        compiler_params=_params(2),
        name="conformer_conv" if conformer else "short_conv",
    )(*args)
    return out.reshape(t, e)


def _nt_dot(a, b):
    return lax.dot_general(a, b, (((1,), (1,)), ((), ())), preferred_element_type=F32)


def _ctx_attn_kernel(n_heads, q_ref, k_ref, v_ref, sz_ref, o_ref):
    for hh in range(n_heads):
        sl = slice(hh * HEAD_DIM, (hh + 1) * HEAD_DIM)
        s = _nt_dot(q_ref[:, sl], k_ref[:, sl])
        p = jnp.exp(s - jnp.max(s, axis=-1, keepdims=True))
        l = jnp.sum(p, axis=-1, keepdims=True)
        o = jnp.dot(p.astype(BF16), v_ref[:, sl], preferred_element_type=F32) / l
        o_ref[:, sl] = (o * sz_ref[:, sl].astype(F32)).astype(BF16)


def _ctx_attention(q, k, v, sz, nb, seq):
    t, e = q.shape
    heads_per_step = 8
    w = heads_per_step * HEAD_DIM
    blk = pl.BlockSpec((seq, w), lambda b, g: (b, g))
    return pl.pallas_call(
        functools.partial(_ctx_attn_kernel, heads_per_step),
        grid=(nb, e // w),
        in_specs=[blk] * 4,
        out_specs=blk,
        out_shape=jax.ShapeDtypeStruct((t, e), BF16),
        compiler_params=_params(2),
        name="ctx_attention",
    )(q, k, v, sz)


def _nbr_geometry(rows):
    n_blocks = rows // Q_ROWS
    classes = {}
    for name, m in (("first", 0), ("mid", 1), ("last", n_blocks - 1)):
        kr0 = int(np.clip(Q_ROWS * m - WIN_H // 2, 0, rows - K_ROWS))
        per_row = []
        for i in range(Q_ROWS):
            r = Q_ROWS * m + i
            r0 = int(np.clip(r - WIN_H // 2, 0, rows - WIN_H))
            a = r0 - kr0
            assert 0 <= a and a + WIN_H <= K_ROWS
            jp0, jp1 = a // 2, (a + WIN_H - 1) // 2
            pairs = []
            for jp in range(jp0, jp1 + 1):
                halves = []
                for j in (2 * jp, 2 * jp + 1):
                    ok = a <= j < a + WIN_H
                    halves.append(kr0 + j - r + WIN_H - 1 if ok else None)
                pairs.append(tuple(halves))
            per_row.append((jp0, pairs))
        classes[name] = (kr0 - Q_ROWS * m, per_row)
    return classes


def _nbr_kernel(geom, tab_index, n_bias_rows, n_bias_cols,
                rpb_ref, q_ref, k_ref, v_ref, sz_ref, kc_ref, vc_ref, o_ref,
                half_l, half_r, tabs, s_loc, s_ctx, p_loc, p_ctx, linv, kcb, vcb):
    h = pl.program_id(0)
    b = pl.program_id(1)
    n_blocks = q_ref.shape[0] // (Q_ROWS * GRID_W)
    nq = Q_ROWS * GRID_W
    nk = K_ROWS * GRID_W

    @pl.when(b == 0)
    def _build_bias_tables():
        qc = lax.broadcasted_iota(jnp.int32, (GRID_W, 2 * GRID_W), 0)
        lane = lax.broadcasted_iota(jnp.int32, (GRID_W, 2 * GRID_W), 1)
        kc = lane & (GRID_W - 1)
        right = lane >= GRID_W
        cstart = jnp.clip(qc - WIN_W // 2, 0, GRID_W - WIN_W)
        col_ok = (kc >= cstart) & (kc < cstart + WIN_W)
        dcol = kc - qc + (WIN_W - 1)
        code_l = jnp.where(col_ok & jnp.logical_not(right), dcol, -1)
        code_r = jnp.where(col_ok & right, dcol, -1)
        base = h * (n_bias_rows * n_bias_cols)
        for d in range(n_bias_rows):
            acc_l = jnp.full((GRID_W, 2 * GRID_W), NEG_INF, F32)
            acc_r = acc_l
            for dc in range(n_bias_cols):
                val = rpb_ref[base + d * n_bias_cols + dc]
                acc_l = jnp.where(code_l == dc, val, acc_l)
                acc_r = jnp.where(code_r == dc, val, acc_r)
            half_l[d] = acc_l
            half_r[d] = acc_r
        for (dl, dr), t in tab_index.items():
            if dl is not None and dr is not None:
                tabs[t] = jnp.maximum(half_l[dl], half_r[dr])
            elif dl is not None:
                tabs[t] = half_l[dl]
            else:
                tabs[t] = half_r[dr]

    kcb[...] = kc_ref[...].astype(BF16)
    vcb[...] = vc_ref[...].astype(BF16)

    n_ctx = kcb.shape[0] // LANES

    def geometry(m):
        cls = "first" if m == 0 else ("last" if m == n_blocks - 1 else "mid")
        k_off, per_row = geom[cls]
        return m * nq, m * nq + k_off * GRID_W, per_row

    def scores(m):
        q0, k0, _ = geometry(m)
        q = q_ref[q0:q0 + nq, :]
        s_loc[m % 2] = _nt_dot(q, k_ref[k0:k0 + nk, :])
        s_ctx[m % 2] = _nt_dot(q, kcb[...])

    def softmax(m):
        _, _, per_row = geometry(m)
        sl_ref, sc_ref, pl_ref, pc_ref = s_loc.at[m % 2], s_ctx.at[m % 2], p_loc.at[m % 2], p_ctx.at[m % 2]
        for i in range(Q_ROWS):
            rs = slice(i * GRID_W, (i + 1) * GRID_W)
            jp0, pairs = per_row[i]
            lo, hi = jp0 * LANES, (jp0 + len(pairs)) * LANES

            def local(t):
                cs = slice(lo + t * LANES, lo + (t + 1) * LANES)
                return cs, sl_ref[rs, cs] + tabs[tab_index[pairs[t]]]

            def context(t):
                cs = slice(t * LANES, (t + 1) * LANES)
                return cs, sc_ref[rs, cs]

            vmax = context(0)[1]
            for t in range(1, n_ctx):
                vmax = jnp.maximum(vmax, context(t)[1])
            for t in range(len(pairs)):
                vmax = jnp.maximum(vmax, local(t)[1])
            mx = jnp.max(vmax, axis=-1, keepdims=True)
            vsum = jnp.zeros((GRID_W, LANES), F32)
            for t in range(n_ctx):
                cs, x = context(t)
                px = jnp.exp(x - mx)
                vsum = vsum + px
                pc_ref[rs, cs] = px.astype(BF16)
            for t in range(len(pairs)):
                cs, x = local(t)
                px = jnp.exp(x - mx)
                vsum = vsum + px
                pl_ref[rs, cs] = px.astype(BF16)
            if lo > 0:
                pl_ref[rs, 0:lo] = jnp.zeros((GRID_W, lo), BF16)
            if hi < nk:
                pl_ref[rs, hi:nk] = jnp.zeros((GRID_W, nk - hi), BF16)
            l = jnp.sum(vsum, axis=-1, keepdims=True)
            linv[m % 2, rs, :] = jnp.broadcast_to(1.0 / l, (GRID_W, HEAD_DIM))

    def values(m):
        q0, k0, _ = geometry(m)
        o = jnp.dot(p_loc[m % 2], v_ref[k0:k0 + nk, :], preferred_element_type=F32)
        o = o + jnp.dot(p_ctx[m % 2], vcb[...], preferred_element_type=F32)
        o = o * linv[m % 2] * sz_ref[q0:q0 + nq, :].astype(F32)
        o_ref[q0:q0 + nq, :] = o.astype(BF16)

    scores(0)
    for m in range(n_blocks):
        if m + 1 < n_blocks:
            scores(m + 1)
        softmax(m)
        values(m)


def _nbr_attention(q, k, v, sz, cache_k, cache_v, j, rpb):
    t, e = q.shape
    nb, _, past, n_heads, hd = cache_k.shape
    seq = t // nb
    rows = seq // GRID_W
    assert hd == HEAD_DIM and e == n_heads * HEAD_DIM and seq == rows * GRID_W
    assert rows % Q_ROWS == 0 and rows >= 3 * Q_ROWS and GRID_W * 2 == LANES
    n_bias_rows, n_bias_cols = rpb.shape[1], rpb.shape[2]
    assert n_bias_rows == 2 * WIN_H - 1 and n_bias_cols == 2 * WIN_W - 1
    geom = _nbr_geometry(rows)
    keys = sorted({pr for _, per_row in geom.values() for _, pairs in per_row for pr in pairs},
                  key=lambda pr: (pr[0] is None, pr[1] is None, pr))
    tab_index = {pr: n for n, pr in enumerate(keys)}
    nq, nk = Q_ROWS * GRID_W, K_ROWS * GRID_W
    head = pl.BlockSpec((seq, HEAD_DIM), lambda h, b: (b, h))
    ctx = pl.BlockSpec((None, None, past, HEAD_DIM), lambda h, b: (b, j, 0, h))
    ck = cache_k.reshape(nb, cache_k.shape[1], past, e)
    cv = cache_v.reshape(nb, cache_v.shape[1], past, e)
    return pl.pallas_call(
        functools.partial(_nbr_kernel, geom, tab_index, n_bias_rows, n_bias_cols),
        grid=(n_heads, nb),
        in_specs=[pl.BlockSpec(memory_space=pltpu.SMEM), head, head, head, head, ctx, ctx],
        out_specs=head,
        out_shape=jax.ShapeDtypeStruct((t, e), BF16),
        scratch_shapes=[pltpu.VMEM((n_bias_rows, GRID_W, LANES), F32),
                        pltpu.VMEM((n_bias_rows, GRID_W, LANES), F32),
                        pltpu.VMEM((len(keys), GRID_W, LANES), F32),
                        pltpu.VMEM((2, nq, nk), F32),
                        pltpu.VMEM((2, nq, past), F32),
                        pltpu.VMEM((2, nq, nk), BF16),
                        pltpu.VMEM((2, nq, past), BF16),
                        pltpu.VMEM((2, nq, HEAD_DIM), F32),
                        pltpu.VMEM((past, HEAD_DIM), BF16),
                        pltpu.VMEM((past, HEAD_DIM), BF16)],
        compiler_params=_params(2),
        name="nbr_attention",
    )(rpb.reshape(-1), q, k, v, sz, ck, cv)


def kernel(x_prompt, x_sample, cache_k, cache_v, c, c_ctx, w_ada, b_ada, g_pre, g_post,
           w_in_a, conv_w_a, conv_b_a, ln_g_a, ln_b_a, w_in_b, rpb_b, w_in_c, conv_w_c, w_out):
    depth = w_ada.shape[0]
    d = x_prompt.shape[-1]
    tm = 512
    mods = _modulation(c, c_ctx, w_ada, b_ada)
    w_out_bf = w_out.astype(BF16)

    groups = []
    for x, is_ctx in ((x_prompt, True), (x_sample, False)):
        nb, seq, _ = x.shape
        if is_ctx:
            row_fn = lambda i: 0
        else:
            row_fn = functools.partial(lambda i, per: 1 + i // per, per=seq // tm)
        x2 = x.reshape(nb * seq, d)
        groups.append(dict(x=x2, h=_prenorm(x2, mods, g_pre, 0, row_fn, tm),
                           nb=nb, seq=seq, row_fn=row_fn, is_ctx=is_ctx))

    new_k, new_v = [], []
    for i in range(depth):
        kind, j = i % N_MIXERS, i // N_MIXERS
        for grp in groups:
            nb, seq = grp["nb"], grp["seq"]
            if kind == 0:
                u, sz = _inproj(grp["h"], w_in_a, j, 0)
                g = _conv_mixer(u, sz, nb, seq, conv_w_a[j], True, conv_b_a[j], ln_g_a[j], ln_b_a[j])
            elif kind == 1:
                if grp["is_ctx"]:
                    q, k, v, sz, k32, v32 = _inproj(grp["h"], w_in_b, j, 1, emit_kv=True)
                    g = _ctx_attention(q, k, v, sz, nb, seq)
                    new_k.append(k32.reshape(nb, seq, -1, HEAD_DIM))
                    new_v.append(v32.reshape(nb, seq, -1, HEAD_DIM))
                else:
                    q, k, v, sz = _inproj(grp["h"], w_in_b, j, 1)
                    g = _nbr_attention(q, k, v, sz, cache_k, cache_v, j, rpb_b[j])
            else:
                cv, bz = _inproj(grp["h"], w_in_c, j, 2)
                g = _conv_mixer(cv, bz, nb, seq, conv_w_c[j], False)
            grp["x"], grp["h"] = _outproj(g, w_out_bf, i, grp["x"], mods, g_post, g_pre, grp["row_fn"], tm)

    y_prompt = groups[0]["x"].reshape(x_prompt.shape)
    y_sample = groups[1]["x"].reshape(x_sample.shape)
    return (y_prompt, y_sample, jnp.stack(new_k, axis=1), jnp.stack(new_v, axis=1))
```

```python
import functools

import numpy as np
import jax
import jax.numpy as jnp
from jax import lax
from jax.experimental import pallas as pl
from jax.experimental.pallas import tpu as pltpu

F32 = jnp.float32
BF16 = jnp.bfloat16

GRID_W = 64
HEAD_DIM = 128
WIN_H = 8
WIN_W = 16
N_MIXERS = 3
EPS = 1e-6
NEG_INF = -1e30
ATTN_SCALE = HEAD_DIM ** -0.5
LOG2_E = 1.4426950408889634

LANES = 128
BF16_ROWS = 16
HALO = BF16_ROWS
CONV_WORDS = 64
MOD_ROWS = 8
VMEM_LIMIT = 56 * 1024 * 1024
Q_ROWS = 8
K_ROWS = 16


def _params(n_axes):
    return pltpu.CompilerParams(dimension_semantics=("arbitrary",) * n_axes,
                                vmem_limit_bytes=VMEM_LIMIT)


def _sigmoid(x):
    return 1.0 / (1.0 + jnp.exp2(x * (-LOG2_E)))


def _silu(x):
    return x * _sigmoid(x)


def _mod_kernel(cond_ref, w_ref, b_ref, o_ref):
    s = _silu(cond_ref[...])
    o_ref[...] = jnp.dot(s, w_ref[...], preferred_element_type=F32,
                         precision=lax.Precision.HIGHEST) + b_ref[...]


def _modulation(c, c_ctx, w_ada, b_ada):
    depth, d, d3 = w_ada.shape
    pad = MOD_ROWS - 1 - c.shape[0]
    cond = jnp.concatenate([c_ctx[None, :], c, jnp.zeros((pad, d), F32)], axis=0)
    tn = 1024
    mods = pl.pallas_call(
        _mod_kernel,
        grid=(depth, d3 // tn),
        in_specs=[pl.BlockSpec((MOD_ROWS, d), lambda l, n: (0, 0)),
                  pl.BlockSpec((None, d, tn), lambda l, n: (l, 0, n)),
                  pl.BlockSpec((None, 1, tn), lambda l, n: (l, 0, n))],
        out_specs=pl.BlockSpec((None, MOD_ROWS, tn), lambda l, n: (l, 0, n)),
        out_shape=jax.ShapeDtypeStruct((depth, MOD_ROWS, d3), F32),
        compiler_params=_params(2),
        name="modulation",
    )(cond, w_ada, b_ada.reshape(depth, 1, d3))
    return mods.reshape(depth * MOD_ROWS, 1, d3)


def _mod_specs(layer, d, row_fn):
    def spec(col):
        return pl.BlockSpec((None, 1, d), lambda i: (layer * MOD_ROWS + row_fn(i), 0, col))
    return spec(0), spec(1), spec(2)


def _adaln(x, g, shift, scale):
    y = x * lax.rsqrt(jnp.mean(x * x, axis=-1, keepdims=True) + EPS)
    return (y * g) * (1.0 + scale) + shift


def _prenorm_kernel(x_ref, g_ref, sh_ref, sc_ref, h_ref):
    h_ref[...] = _adaln(x_ref[...], g_ref[...], sh_ref[...], sc_ref[...]).astype(BF16)


def _prenorm(x, mods, g_pre, layer, row_fn, tm):
    t, d = x.shape
    sh, sc, _ = _mod_specs(layer, d, row_fn)
    return pl.pallas_call(
        _prenorm_kernel,
        grid=(t // tm,),
        in_specs=[pl.BlockSpec((tm, d), lambda i: (i, 0)),
                  pl.BlockSpec((None, 1, d), lambda i: (layer, 0, 0)),
                  sh, sc],
        out_specs=pl.BlockSpec((tm, d), lambda i: (i, 0)),
        out_shape=jax.ShapeDtypeStruct((t, d), BF16),
        compiler_params=_params(1),
        name="prenorm",
    )(x, g_pre.reshape(g_pre.shape[0], 1, d), mods, mods)


def _inproj_kernel(kind, n_groups, tn, emit_kv, h_ref, *refs):
    w_refs = refs[:n_groups]
    out_refs = refs[n_groups:-1]
    wbf_ref = refs[-1]

    @pl.when(pl.program_id(1) == 0)
    def _cast_weights():
        for g in range(n_groups):
            wbf_ref[:, g * tn:(g + 1) * tn] = w_refs[g][...].astype(BF16)

    r = jnp.dot(h_ref[...], wbf_ref[...], preferred_element_type=F32)
    p = [r[:, g * tn:(g + 1) * tn] for g in range(n_groups)]
    if kind == 0:
        a, b, z = p
        out_refs[0][...] = (a * _sigmoid(b)).astype(BF16)
        out_refs[1][...] = _silu(z).astype(BF16)
    elif kind == 1:
        q, k, v, z = p
        out_refs[0][...] = (q * ATTN_SCALE).astype(BF16)
        out_refs[1][...] = k.astype(BF16)
        out_refs[2][...] = v.astype(BF16)
        out_refs[3][...] = _silu(z).astype(BF16)
        if emit_kv:
            out_refs[4][...] = k
            out_refs[5][...] = v
    else:
        bg, cg, v, z = p
        out_refs[0][...] = (cg * v).astype(BF16)
        out_refs[1][...] = (bg * _silu(z)).astype(BF16)


def _inproj(h, w_in, j, kind, emit_kv=False):
    t, d = h.shape
    n_groups = 3 if kind == 0 else 4
    e = w_in.shape[-1] // n_groups
    tm = min(1024 if emit_kv else 2048, t)
    tn = 256
    nj = e // tn
    w_specs = [pl.BlockSpec((None, d, tn), functools.partial(lambda jj, ii, g: (j, 0, g * nj + jj), g=g))
               for g in range(n_groups)]
    n_bf = {0: 2, 1: 4, 2: 2}[kind]
    out_shape = [jax.ShapeDtypeStruct((t, e), BF16)] * n_bf
    if emit_kv:
        out_shape = out_shape + [jax.ShapeDtypeStruct((t, e), F32)] * 2
    out_spec = pl.BlockSpec((tm, tn), lambda jj, ii: (ii, jj))
    return pl.pallas_call(
        functools.partial(_inproj_kernel, kind, n_groups, tn, emit_kv),
        grid=(nj, t // tm),
        in_specs=[pl.BlockSpec((tm, d), lambda jj, ii: (ii, 0))] + w_specs,
        out_specs=[out_spec] * len(out_shape),
        out_shape=out_shape,
        scratch_shapes=[pltpu.VMEM((d, n_groups * tn), BF16)],
        compiler_params=_params(2),
        name="inproj",
    )(h, *([w_in] * n_groups))


def _outproj_kernel(has_next, g_ref, w_ref, x_ref, gpost_ref, gate_ref, *refs):
    m = jnp.dot(g_ref[...], w_ref[...], preferred_element_type=F32)
    y = m * lax.rsqrt(jnp.mean(m * m, axis=-1, keepdims=True) + EPS)
    xn = x_ref[...] + gate_ref[...] * (y * gpost_ref[...])
    if has_next:
        gpre_ref, sh_ref, sc_ref, xo_ref, h_ref = refs
        h_ref[...] = _adaln(xn, gpre_ref[...], sh_ref[...], sc_ref[...]).astype(BF16)
    else:
        (xo_ref,) = refs
    xo_ref[...] = xn


def _outproj(g, w_out_bf, layer, x, mods, g_post, g_pre, row_fn, tm):
    t, e = g.shape
    d = x.shape[1]
    depth = w_out_bf.shape[0]
    has_next = layer + 1 < depth
    _, _, gate = _mod_specs(layer, d, row_fn)
    row = lambda i: (i, 0)
    in_specs = [pl.BlockSpec((tm, e), row),
                pl.BlockSpec((None, e, d), lambda i: (layer, 0, 0), pipeline_mode=pl.Buffered(1)),
                pl.BlockSpec((tm, d), row),
                pl.BlockSpec((None, 1, d), lambda i: (layer, 0, 0)),
                gate]
    args = [g, w_out_bf, x, g_post.reshape(depth, 1, d), mods]
    out_specs = [pl.BlockSpec((tm, d), row)]
    out_shape = [jax.ShapeDtypeStruct((t, d), F32)]
    if has_next:
        sh, sc, _ = _mod_specs(layer + 1, d, row_fn)
        in_specs += [pl.BlockSpec((None, 1, d), lambda i: (layer + 1, 0, 0)), sh, sc]
        args += [g_pre.reshape(depth, 1, d), mods, mods]
        out_specs.append(pl.BlockSpec((tm, d), row))
        out_shape.append(jax.ShapeDtypeStruct((t, d), BF16))
    outs = pl.pallas_call(
        functools.partial(_outproj_kernel, has_next),
        grid=(t // tm,),
        in_specs=in_specs,
        out_specs=out_specs,
        out_shape=out_shape,
        compiler_params=_params(1),
        name="outproj",
    )(*args)
    return (outs[0], outs[1]) if has_next else (outs[0], None)


def _conv_kernel(taps, conformer, ts, n_col, u_ref, up_ref, un_ref, aux_ref, w_ref, *refs):
    if conformer:
        cb_ref, lg_ref, lb_ref, o_ref, wpk, wb, cbuf = refs
    else:
        o_ref, wpk, wb, cbuf = refs
    s = pl.program_id(1)
    has_prev = s > 0
    has_next = s < pl.num_programs(1) - 1
    pad = taps // 2
    half = ts // 2

    @pl.when((pl.program_id(0) == 0) & (s == 0))
    def _broadcast_weights():
        def body(c, carry):
            for k in range(taps):
                wb[c, k] = jnp.broadcast_to(w_ref[c, k:k + 1, :], (BF16_ROWS, LANES)).astype(BF16)
            return carry
        lax.fori_loop(0, n_col, body, 0)

    def pack(lo, hi):
        return pltpu.pack_elementwise([lo.astype(F32), hi.astype(F32)], packed_dtype=BF16)

    for c in range(n_col):
        sl = slice(c * LANES, (c + 1) * LANES)
        prev = jnp.where(has_prev, up_ref[:, sl].astype(F32), 0.0)
        nxt = jnp.where(has_next, un_ref[:, sl].astype(F32), 0.0)
        wpk[c, 0:HALO, :] = pack(prev, u_ref[half - HALO:half, sl])
        wpk[c, HALO:HALO + half, :] = pack(u_ref[0:half, sl], u_ref[half:ts, sl])
        wpk[c, HALO + half:2 * HALO + half, :] = pack(u_ref[half:half + HALO, sl], nxt)

    def col_body(c, carry):
        def chunk_body(j, carry2):
            w0 = pl.multiple_of(j * CONV_WORDS, CONV_WORDS)
            acc = None
            for k in range(taps):
                x = pltpu.bitcast(wpk[c, pl.ds(w0 + (HALO - pad + k), CONV_WORDS), :], BF16)
                wv = pltpu.repeat(wb[c, k], 2 * CONV_WORDS // BF16_ROWS, axis=0)
                term = x.astype(F32) * wv.astype(F32)
                acc = term if acc is None else acc + term
            if conformer:
                acc = acc + cb_ref[c]
            cbuf[c, pl.ds(pl.multiple_of(2 * w0, 2 * CONV_WORDS), 2 * CONV_WORDS), :] = acc
            return carry2
        lax.fori_loop(0, half // CONV_WORDS, chunk_body, 0)
        return carry

    lax.fori_loop(0, n_col, col_body, 0)

    if conformer:
        e = n_col * LANES
        tot = cbuf[0]
        for c in range(1, n_col):
            tot = tot + cbuf[c]
        mu = jnp.sum(tot, axis=-1, keepdims=True) / e
        d0 = cbuf[0] - mu
        tot = d0 * d0
        for c in range(1, n_col):
            dc = cbuf[c] - mu
            tot = tot + dc * dc
        rstd = lax.rsqrt(jnp.sum(tot, axis=-1, keepdims=True) / e + EPS)
        for c in range(n_col):
            cbuf[c] = _silu((cbuf[c] - mu) * rstd * lg_ref[c] + lb_ref[c])
    for c in range(n_col):
        sl = slice(c * LANES, (c + 1) * LANES)
        for i in range(2):
            y = cbuf[c, pl.ds(i, half, stride=2), :]
            rows = slice(i * half, (i + 1) * half)
            o_ref[rows, sl] = (y * aux_ref[rows, sl].astype(F32)).astype(BF16)


def _cols(v, n_col):
    lead = v.shape[:-1]
    v = v.reshape(lead + (n_col, LANES))
    return jnp.moveaxis(v, -2, 0)


def _conv_mixer(u, aux, nb, seq, conv_w, conformer, conv_b=None, ln_g=None, ln_b=None):
    t, e = u.shape
    taps = conv_w.shape[0]
    n_col = e // LANES
    ts = 256
    ns = seq // ts
    hb = ts // HALO
    n_halo = seq // HALO
    u3 = u.reshape(nb, seq, e)
    aux3 = aux.reshape(nb, seq, e)
    tile = pl.BlockSpec((None, ts, e), lambda b, s: (b, s, 0))
    prev = pl.BlockSpec((None, HALO, e), lambda b, s: (b, jnp.maximum(s * hb - 1, 0), 0))
    nxt = pl.BlockSpec((None, HALO, e), lambda b, s: (b, jnp.minimum((s + 1) * hb, n_halo - 1), 0))
    in_specs = [tile, prev, nxt, tile,
                pl.BlockSpec((n_col, taps, LANES), lambda b, s: (0, 0, 0))]
    args = [u3, u3, u3, aux3, _cols(conv_w, n_col)]
    if conformer:
        vec = pl.BlockSpec((n_col, 1, LANES), lambda b, s: (0, 0, 0))
        in_specs += [vec, vec, vec]
        args += [_cols(conv_b[None, :], n_col), _cols(ln_g[None, :], n_col), _cols(ln_b[None, :], n_col)]
    scratch = [pltpu.VMEM((n_col, ts // 2 + 2 * HALO, LANES), jnp.uint32),
               pltpu.VMEM((n_col, taps, BF16_ROWS, LANES), BF16),
               pltpu.VMEM((n_col, ts, LANES), F32)]
    out = pl.pallas_call(
        functools.partial(_conv_kernel, taps, conformer, ts, n_col),
        grid=(nb, ns),
        in_specs=in_specs,
        out_specs=tile,
        out_shape=jax.ShapeDtypeStruct((nb, seq, e), BF16),
        scratch_shapes=scratch,
        compiler_params=_params(2),
        name="conformer_conv" if conformer else "short_conv",
    )(*args)
    return out.reshape(t, e)


def _nt_dot(a, b):
    return lax.dot_general(a, b, (((1,), (1,)), ((), ())), preferred_element_type=F32)


def _ctx_attn_kernel(n_heads, q_ref, k_ref, v_ref, sz_ref, o_ref):
    for hh in range(n_heads):
        sl = slice(hh * HEAD_DIM, (hh + 1) * HEAD_DIM)
        s = _nt_dot(q_ref[:, sl], k_ref[:, sl])
        p = jnp.exp(s - jnp.max(s, axis=-1, keepdims=True))
        l = jnp.sum(p, axis=-1, keepdims=True)
        o = jnp.dot(p.astype(BF16), v_ref[:, sl], preferred_element_type=F32) / l
        o_ref[:, sl] = (o * sz_ref[:, sl].astype(F32)).astype(BF16)


def _ctx_attention(q, k, v, sz, nb, seq):
    t, e = q.shape
    heads_per_step = 8
    w = heads_per_step * HEAD_DIM
    blk = pl.BlockSpec((seq, w), lambda b, g: (b, g))
    return pl.pallas_call(
        functools.partial(_ctx_attn_kernel, heads_per_step),
        grid=(nb, e // w),
        in_specs=[blk] * 4,
        out_specs=blk,
        out_shape=jax.ShapeDtypeStruct((t, e), BF16),
        compiler_params=_params(2),
        name="ctx_attention",
    )(q, k, v, sz)


def _nbr_geometry(rows):
    n_blocks = rows // Q_ROWS
    classes = {}
    for name, m in (("first", 0), ("mid", 1), ("last", n_blocks - 1)):
        kr0 = int(np.clip(Q_ROWS * m - WIN_H // 2, 0, rows - K_ROWS))
        per_row = []
        for i in range(Q_ROWS):
            r = Q_ROWS * m + i
            r0 = int(np.clip(r - WIN_H // 2, 0, rows - WIN_H))
            a = r0 - kr0
            assert 0 <= a and a + WIN_H <= K_ROWS
            jp0, jp1 = a // 2, (a + WIN_H - 1) // 2
            pairs = []
            for jp in range(jp0, jp1 + 1):
                halves = []
                for j in (2 * jp, 2 * jp + 1):
                    ok = a <= j < a + WIN_H
                    halves.append(kr0 + j - r + WIN_H - 1 if ok else None)
                pairs.append(tuple(halves))
            per_row.append((jp0, pairs))
        classes[name] = (kr0 - Q_ROWS * m, per_row)
    return classes


def _nbr_kernel(geom, tab_index, n_bias_rows, n_bias_cols,
                rpb_ref, q_ref, k_ref, v_ref, sz_ref, kc_ref, vc_ref, o_ref,
                half_l, half_r, tabs, s_loc, s_ctx, p_loc, p_ctx, linv, kcb, vcb):
    h = pl.program_id(0)
    b = pl.program_id(1)
    n_blocks = q_ref.shape[0] // (Q_ROWS * GRID_W)
    nq = Q_ROWS * GRID_W
    nk = K_ROWS * GRID_W

    @pl.when(b == 0)
    def _build_bias_tables():
        qc = lax.broadcasted_iota(jnp.int32, (GRID_W, 2 * GRID_W), 0)
        lane = lax.broadcasted_iota(jnp.int32, (GRID_W, 2 * GRID_W), 1)
        kc = lane & (GRID_W - 1)
        right = lane >= GRID_W
        cstart = jnp.clip(qc - WIN_W // 2, 0, GRID_W - WIN_W)
        col_ok = (kc >= cstart) & (kc < cstart + WIN_W)
        dcol = kc - qc + (WIN_W - 1)
        code_l = jnp.where(col_ok & jnp.logical_not(right), dcol, -1)
        code_r = jnp.where(col_ok & right, dcol, -1)
        base = h * (n_bias_rows * n_bias_cols)
        for d in range(n_bias_rows):
            acc_l = jnp.full((GRID_W, 2 * GRID_W), NEG_INF, F32)
            acc_r = acc_l
            for dc in range(n_bias_cols):
                val = rpb_ref[base + d * n_bias_cols + dc]
                acc_l = jnp.where(code_l == dc, val, acc_l)
                acc_r = jnp.where(code_r == dc, val, acc_r)
            half_l[d] = acc_l
            half_r[d] = acc_r
        for (dl, dr), t in tab_index.items():
            if dl is not None and dr is not None:
                tabs[t] = jnp.maximum(half_l[dl], half_r[dr])
            elif dl is not None:
                tabs[t] = half_l[dl]
            else:
                tabs[t] = half_r[dr]

    kcb[...] = kc_ref[...].astype(BF16)
    vcb[...] = vc_ref[...].astype(BF16)

    n_ctx = kcb.shape[0] // LANES

    def geometry(m):
        cls = "first" if m == 0 else ("last" if m == n_blocks - 1 else "mid")
        k_off, per_row = geom[cls]
        return m * nq, m * nq + k_off * GRID_W, per_row

    def scores(m):
        q0, k0, _ = geometry(m)
        q = q_ref[q0:q0 + nq, :]
        s_loc[m % 2] = _nt_dot(q, k_ref[k0:k0 + nk, :])
        s_ctx[m % 2] = _nt_dot(q, kcb[...])

    def softmax(m):
        _, _, per_row = geometry(m)
        sl_ref, sc_ref, pl_ref, pc_ref = s_loc.at[m % 2], s_ctx.at[m % 2], p_loc.at[m % 2], p_ctx.at[m % 2]
        for i in range(Q_ROWS):
            rs = slice(i * GRID_W, (i + 1) * GRID_W)
            jp0, pairs = per_row[i]
            lo, hi = jp0 * LANES, (jp0 + len(pairs)) * LANES

            def local(t):
                cs = slice(lo + t * LANES, lo + (t + 1) * LANES)
                return cs, sl_ref[rs, cs] + tabs[tab_index[pairs[t]]]

            def context(t):
                cs = slice(t * LANES, (t + 1) * LANES)
                return cs, sc_ref[rs, cs]

            vmax = context(0)[1]
            for t in range(1, n_ctx):
                vmax = jnp.maximum(vmax, context(t)[1])
            for t in range(len(pairs)):
                vmax = jnp.maximum(vmax, local(t)[1])
            mx = jnp.max(vmax, axis=-1, keepdims=True)
            vsum = jnp.zeros((GRID_W, LANES), F32)
            for t in range(n_ctx):
                cs, x = context(t)
                px = jnp.exp(x - mx)
                vsum = vsum + px
                pc_ref[rs, cs] = px.astype(BF16)
            for t in range(len(pairs)):
                cs, x = local(t)
                px = jnp.exp(x - mx)
                vsum = vsum + px
                pl_ref[rs, cs] = px.astype(BF16)
            if lo > 0:
                pl_ref[rs, 0:lo] = jnp.zeros((GRID_W, lo), BF16)
            if hi < nk:
                pl_ref[rs, hi:nk] = jnp.zeros((GRID_W, nk - hi), BF16)
            l = jnp.sum(vsum, axis=-1, keepdims=True)
            linv[m % 2, rs, :] = jnp.broadcast_to(1.0 / l, (GRID_W, HEAD_DIM))

    def values(m):
        q0, k0, _ = geometry(m)
        o = jnp.dot(p_loc[m % 2], v_ref[k0:k0 + nk, :], preferred_element_type=F32)
        o = o + jnp.dot(p_ctx[m % 2], vcb[...], preferred_element_type=F32)
        o = o * linv[m % 2] * sz_ref[q0:q0 + nq, :].astype(F32)
        o_ref[q0:q0 + nq, :] = o.astype(BF16)

    scores(0)
    for m in range(n_blocks):
        if m + 1 < n_blocks:
            scores(m + 1)
        softmax(m)
        values(m)


def _nbr_attention(q, k, v, sz, cache_k, cache_v, j, rpb):
    t, e = q.shape
    nb, _, past, n_heads, hd = cache_k.shape
    seq = t // nb
    rows = seq // GRID_W
    assert hd == HEAD_DIM and e == n_heads * HEAD_DIM and seq == rows * GRID_W
    assert rows % Q_ROWS == 0 and rows >= 3 * Q_ROWS and GRID_W * 2 == LANES
    n_bias_rows, n_bias_cols = rpb.shape[1], rpb.shape[2]
    assert n_bias_rows == 2 * WIN_H - 1 and n_bias_cols == 2 * WIN_W - 1
    geom = _nbr_geometry(rows)
    keys = sorted({pr for _, per_row in geom.values() for _, pairs in per_row for pr in pairs},
                  key=lambda pr: (pr[0] is None, pr[1] is None, pr))
    tab_index = {pr: n for n, pr in enumerate(keys)}
    nq, nk = Q_ROWS * GRID_W, K_ROWS * GRID_W
    head = pl.BlockSpec((seq, HEAD_DIM), lambda h, b: (b, h))
    ctx = pl.BlockSpec((None, None, past, HEAD_DIM), lambda h, b: (b, j, 0, h))
    ck = cache_k.reshape(nb, cache_k.shape[1], past, e)
    cv = cache_v.reshape(nb, cache_v.shape[1], past, e)
    return pl.pallas_call(
        functools.partial(_nbr_kernel, geom, tab_index, n_bias_rows, n_bias_cols),
        grid=(n_heads, nb),
        in_specs=[pl.BlockSpec(memory_space=pltpu.SMEM), head, head, head, head, ctx, ctx],
        out_specs=head,
        out_shape=jax.ShapeDtypeStruct((t, e), BF16),
        scratch_shapes=[pltpu.VMEM((n_bias_rows, GRID_W, LANES), F32),
                        pltpu.VMEM((n_bias_rows, GRID_W, LANES), F32),
                        pltpu.VMEM((len(keys), GRID_W, LANES), F32),
                        pltpu.VMEM((2, nq, nk), F32),
                        pltpu.VMEM((2, nq, past), F32),
                        pltpu.VMEM((2, nq, nk), BF16),
                        pltpu.VMEM((2, nq, past), BF16),
                        pltpu.VMEM((2, nq, HEAD_DIM), F32),
                        pltpu.VMEM((past, HEAD_DIM), BF16),
                        pltpu.VMEM((past, HEAD_DIM), BF16)],
        compiler_params=_params(2),
        name="nbr_attention",
    )(rpb.reshape(-1), q, k, v, sz, ck, cv)


def kernel(x_prompt, x_sample, cache_k, cache_v, c, c_ctx, w_ada, b_ada, g_pre, g_post,
           w_in_a, conv_w_a, conv_b_a, ln_g_a, ln_b_a, w_in_b, rpb_b, w_in_c, conv_w_c, w_out):
    depth = w_ada.shape[0]
    d = x_prompt.shape[-1]
    tm = 512
    tm_pre = 1024
    mods = _modulation(c, c_ctx, w_ada, b_ada)
    w_out_bf = w_out.astype(BF16)

    groups = []
    for x, is_ctx in ((x_prompt, True), (x_sample, False)):
        nb, seq, _ = x.shape
        if is_ctx:
            row_fn = lambda tile: (lambda i: 0)
        else:
            row_fn = functools.partial(lambda tile, per: (lambda i: 1 + i // (per // tile)), per=seq)
        x2 = x.reshape(nb * seq, d)
        groups.append(dict(x=x2, h=_prenorm(x2, mods, g_pre, 0, row_fn(tm_pre), tm_pre),
                           nb=nb, seq=seq, row_fn=row_fn(tm), is_ctx=is_ctx))

    new_k, new_v = [], []
    for i in range(depth):
        kind, j = i % N_MIXERS, i // N_MIXERS
        for grp in groups:
            nb, seq = grp["nb"], grp["seq"]
            if kind == 0:
                u, sz = _inproj(grp["h"], w_in_a, j, 0)
                g = _conv_mixer(u, sz, nb, seq, conv_w_a[j], True, conv_b_a[j], ln_g_a[j], ln_b_a[j])
            elif kind == 1:
                if grp["is_ctx"]:
                    q, k, v, sz, k32, v32 = _inproj(grp["h"], w_in_b, j, 1, emit_kv=True)
                    g = _ctx_attention(q, k, v, sz, nb, seq)
                    new_k.append(k32.reshape(nb, seq, -1, HEAD_DIM))
                    new_v.append(v32.reshape(nb, seq, -1, HEAD_DIM))
                else:
                    q, k, v, sz = _inproj(grp["h"], w_in_b, j, 1)
                    g = _nbr_attention(q, k, v, sz, cache_k, cache_v, j, rpb_b[j])
            else:
                cv, bz = _inproj(grp["h"], w_in_c, j, 2)
                g = _conv_mixer(cv, bz, nb, seq, conv_w_c[j], False)
            grp["x"], grp["h"] = _outproj(g, w_out_bf, i, grp["x"], mods, g_post, g_pre, grp["row_fn"], tm)

    y_prompt = groups[0]["x"].reshape(x_prompt.shape)
    y_sample = groups[1]["x"].reshape(x_sample.shape)
    return (y_prompt, y_sample, jnp.stack(new_k, axis=1), jnp.stack(new_v, axis=1))
```

```python
import functools

import numpy as np
import jax
import jax.numpy as jnp
from jax import lax
from jax.experimental import pallas as pl
from jax.experimental.pallas import tpu as pltpu

F32 = jnp.float32
BF16 = jnp.bfloat16

GRID_W = 64
HEAD_DIM = 128
WIN_H = 8
WIN_W = 16
N_MIXERS = 3
EPS = 1e-6
NEG_INF = -1e30
ATTN_SCALE = HEAD_DIM ** -0.5
LOG2_E = 1.4426950408889634

LANES = 128
BF16_ROWS = 16
HALO = BF16_ROWS
CONV_WORDS = 64
NORM_ROWS = 32
MOD_ROWS = 8
VMEM_LIMIT = 56 * 1024 * 1024
Q_ROWS = 8
K_ROWS = 16


def _params(n_axes):
    return pltpu.CompilerParams(dimension_semantics=("arbitrary",) * n_axes,
                                vmem_limit_bytes=VMEM_LIMIT)


def _sigmoid(x):
    return 1.0 / (1.0 + jnp.exp2(x * (-LOG2_E)))


def _silu(x):
    return x * _sigmoid(x)


def _mod_kernel(cond_ref, w_ref, b_ref, o_ref):
    s = _silu(cond_ref[...])
    o_ref[...] = jnp.dot(s, w_ref[...], preferred_element_type=F32,
                         precision=lax.Precision.HIGHEST) + b_ref[...]


def _modulation(c, c_ctx, w_ada, b_ada):
    depth, d, d3 = w_ada.shape
    pad = MOD_ROWS - 1 - c.shape[0]
    cond = jnp.concatenate([c_ctx[None, :], c, jnp.zeros((pad, d), F32)], axis=0)
    tn = 1024
    mods = pl.pallas_call(
        _mod_kernel,
        grid=(depth, d3 // tn),
        in_specs=[pl.BlockSpec((MOD_ROWS, d), lambda l, n: (0, 0)),
                  pl.BlockSpec((None, d, tn), lambda l, n: (l, 0, n)),
                  pl.BlockSpec((None, 1, tn), lambda l, n: (l, 0, n))],
        out_specs=pl.BlockSpec((None, MOD_ROWS, tn), lambda l, n: (l, 0, n)),
        out_shape=jax.ShapeDtypeStruct((depth, MOD_ROWS, d3), F32),
        compiler_params=_params(2),
        name="modulation",
    )(cond, w_ada, b_ada.reshape(depth, 1, d3))
    return mods.reshape(depth * MOD_ROWS, 1, d3)


def _mod_specs(layer, d, row_fn):
    def spec(col):
        return pl.BlockSpec((None, 1, d), lambda i: (layer * MOD_ROWS + row_fn(i), 0, col))
    return spec(0), spec(1), spec(2)


def _adaln(x, g, shift, scale):
    y = x * lax.rsqrt(jnp.mean(x * x, axis=-1, keepdims=True) + EPS)
    return (y * g) * (1.0 + scale) + shift


def _prenorm_kernel(x_ref, g_ref, sh_ref, sc_ref, h_ref):
    def body(r, carry):
        rows = pl.ds(pl.multiple_of(r * NORM_ROWS, NORM_ROWS), NORM_ROWS)
        h_ref[rows, :] = _adaln(x_ref[rows, :], g_ref[...], sh_ref[...], sc_ref[...]).astype(BF16)
        return carry
    lax.fori_loop(0, x_ref.shape[0] // NORM_ROWS, body, 0)


def _prenorm(x, mods, g_pre, layer, row_fn, tm):
    t, d = x.shape
    sh, sc, _ = _mod_specs(layer, d, row_fn)
    return pl.pallas_call(
        _prenorm_kernel,
        grid=(t // tm,),
        in_specs=[pl.BlockSpec((tm, d), lambda i: (i, 0)),
                  pl.BlockSpec((None, 1, d), lambda i: (layer, 0, 0)),
                  sh, sc],
        out_specs=pl.BlockSpec((tm, d), lambda i: (i, 0)),
        out_shape=jax.ShapeDtypeStruct((t, d), BF16),
        compiler_params=_params(1),
        name="prenorm",
    )(x, g_pre.reshape(g_pre.shape[0], 1, d), mods, mods)


def _inproj_kernel(kind, n_groups, tn, emit_kv, h_ref, *refs):
    w_refs = refs[:n_groups]
    out_refs = refs[n_groups:-1]
    wbf_ref = refs[-1]

    @pl.when(pl.program_id(1) == 0)
    def _cast_weights():
        for g in range(n_groups):
            wbf_ref[:, g * tn:(g + 1) * tn] = w_refs[g][...].astype(BF16)

    r = jnp.dot(h_ref[...], wbf_ref[...], preferred_element_type=F32)
    p = [r[:, g * tn:(g + 1) * tn] for g in range(n_groups)]
    if kind == 0:
        a, b, z = p
        out_refs[0][...] = (a * _sigmoid(b)).astype(BF16)
        out_refs[1][...] = _silu(z).astype(BF16)
    elif kind == 1:
        q, k, v, z = p
        out_refs[0][...] = (q * ATTN_SCALE).astype(BF16)
        out_refs[1][...] = k.astype(BF16)
        out_refs[2][...] = v.astype(BF16)
        out_refs[3][...] = _silu(z).astype(BF16)
        if emit_kv:
            out_refs[4][...] = k
            out_refs[5][...] = v
    else:
        bg, cg, v, z = p
        out_refs[0][...] = (cg * v).astype(BF16)
        out_refs[1][...] = (bg * _silu(z)).astype(BF16)


def _inproj(h, w_in, j, kind, emit_kv=False):
    t, d = h.shape
    n_groups = 3 if kind == 0 else 4
    e = w_in.shape[-1] // n_groups
    tm = min(1024 if emit_kv else 2048, t)
    tn = 256
    nj = e // tn
    w_specs = [pl.BlockSpec((None, d, tn), functools.partial(lambda jj, ii, g: (j, 0, g * nj + jj), g=g))
               for g in range(n_groups)]
    n_bf = {0: 2, 1: 4, 2: 2}[kind]
    out_shape = [jax.ShapeDtypeStruct((t, e), BF16)] * n_bf
    if emit_kv:
        out_shape = out_shape + [jax.ShapeDtypeStruct((t, e), F32)] * 2
    out_spec = pl.BlockSpec((tm, tn), lambda jj, ii: (ii, jj))
    return pl.pallas_call(
        functools.partial(_inproj_kernel, kind, n_groups, tn, emit_kv),
        grid=(nj, t // tm),
        in_specs=[pl.BlockSpec((tm, d), lambda jj, ii: (ii, 0))] + w_specs,
        out_specs=[out_spec] * len(out_shape),
        out_shape=out_shape,
        scratch_shapes=[pltpu.VMEM((d, n_groups * tn), BF16)],
        compiler_params=_params(2),
        name="inproj",
    )(h, *([w_in] * n_groups))


def _outproj_kernel(has_next, n_tiles, g_ref, w_ref, x_ref, gpost_ref, gate_ref, *refs):
    if has_next:
        gpre_ref, sh_ref, sc_ref, xo_ref, h_ref, m_buf = refs
    else:
        xo_ref, m_buf = refs
    i = pl.program_id(0)

    def matmul(slot):
        m_buf[slot] = jnp.dot(g_ref[...], w_ref[...], preferred_element_type=F32)

    def finish(slot):
        m = m_buf[slot]
        y = m * lax.rsqrt(jnp.mean(m * m, axis=-1, keepdims=True) + EPS)
        xn = x_ref[...] + gate_ref[...] * (y * gpost_ref[...])
        if has_next:
            h_ref[...] = _adaln(xn, gpre_ref[...], sh_ref[...], sc_ref[...]).astype(BF16)
        xo_ref[...] = xn

    @pl.when(i == 0)
    def _first():
        matmul(0)

    for parity in range(2):
        @pl.when((i > 0) & (i < n_tiles) & (i % 2 == parity))
        def _steady():
            matmul(parity)
            finish(1 - parity)

    @pl.when(i == n_tiles)
    def _last():
        finish((n_tiles - 1) % 2)


def _outproj(g, w_out_bf, layer, x, mods, g_post, g_pre, row_fn, tm):
    t, e = g.shape
    d = x.shape[1]
    depth = w_out_bf.shape[0]
    n_tiles = t // tm
    has_next = layer + 1 < depth
    prev = lambda i: jnp.maximum(i - 1, 0)
    prev_row_fn = lambda i: row_fn(prev(i))
    _, _, gate = _mod_specs(layer, d, prev_row_fn)
    row = lambda i: (prev(i), 0)
    in_specs = [pl.BlockSpec((tm, e), lambda i: (jnp.minimum(i, n_tiles - 1), 0)),
                pl.BlockSpec((None, e, d), lambda i: (layer, 0, 0), pipeline_mode=pl.Buffered(1)),
                pl.BlockSpec((tm, d), row),
                pl.BlockSpec((None, 1, d), lambda i: (layer, 0, 0)),
                gate]
    args = [g, w_out_bf, x, g_post.reshape(depth, 1, d), mods]
    out_specs = [pl.BlockSpec((tm, d), row)]
    out_shape = [jax.ShapeDtypeStruct((t, d), F32)]
    if has_next:
        sh, sc, _ = _mod_specs(layer + 1, d, prev_row_fn)
        in_specs += [pl.BlockSpec((None, 1, d), lambda i: (layer + 1, 0, 0)), sh, sc]
        args += [g_pre.reshape(depth, 1, d), mods, mods]
        out_specs.append(pl.BlockSpec((tm, d), row))
        out_shape.append(jax.ShapeDtypeStruct((t, d), BF16))
    outs = pl.pallas_call(
        functools.partial(_outproj_kernel, has_next, n_tiles),
        grid=(n_tiles + 1,),
        in_specs=in_specs,
        out_specs=out_specs,
        out_shape=out_shape,
        scratch_shapes=[pltpu.VMEM((2, tm, d), F32)],
        compiler_params=_params(1),
        name="outproj",
    )(*args)
    return (outs[0], outs[1]) if has_next else (outs[0], None)


def _conv_kernel(taps, conformer, ts, n_col, u_ref, up_ref, un_ref, aux_ref, w_ref, *refs):
    if conformer:
        cb_ref, lg_ref, lb_ref, o_ref, wpk, wb, cbuf = refs
    else:
        o_ref, wpk, wb, cbuf = refs
    s = pl.program_id(1)
    has_prev = s > 0
    has_next = s < pl.num_programs(1) - 1
    pad = taps // 2
    half = ts // 2

    @pl.when((pl.program_id(0) == 0) & (s == 0))
    def _broadcast_weights():
        def body(c, carry):
            for k in range(taps):
                wb[c, k] = jnp.broadcast_to(w_ref[c, k:k + 1, :], (BF16_ROWS, LANES)).astype(BF16)
            return carry
        lax.fori_loop(0, n_col, body, 0)

    def pack(lo, hi):
        return pltpu.pack_elementwise([lo.astype(F32), hi.astype(F32)], packed_dtype=BF16)

    for c in range(n_col):
        sl = slice(c * LANES, (c + 1) * LANES)
        prev = jnp.where(has_prev, up_ref[:, sl].astype(F32), 0.0)
        nxt = jnp.where(has_next, un_ref[:, sl].astype(F32), 0.0)
        wpk[c, 0:HALO, :] = pack(prev, u_ref[half - HALO:half, sl])
        wpk[c, HALO:HALO + half, :] = pack(u_ref[0:half, sl], u_ref[half:ts, sl])
        wpk[c, HALO + half:2 * HALO + half, :] = pack(u_ref[half:half + HALO, sl], nxt)

    def col_body(c, carry):
        def chunk_body(j, carry2):
            w0 = pl.multiple_of(j * CONV_WORDS, CONV_WORDS)
            acc = None
            for k in range(taps):
                x = pltpu.bitcast(wpk[c, pl.ds(w0 + (HALO - pad + k), CONV_WORDS), :], BF16)
                wv = jnp.concatenate([wb[c, k]] * (2 * CONV_WORDS // BF16_ROWS), axis=0)
                term = x.astype(F32) * wv.astype(F32)
                acc = term if acc is None else acc + term
            if conformer:
                acc = acc + cb_ref[c]
            cbuf[c, pl.ds(pl.multiple_of(2 * w0, 2 * CONV_WORDS), 2 * CONV_WORDS), :] = acc
            return carry2
        lax.fori_loop(0, half // CONV_WORDS, chunk_body, 0)
        return carry

    lax.fori_loop(0, n_col, col_body, 0)

    if conformer:
        e = n_col * LANES
        tot = cbuf[0]
        for c in range(1, n_col):
            tot = tot + cbuf[c]
        mu = jnp.sum(tot, axis=-1, keepdims=True) / e
        d0 = cbuf[0] - mu
        tot = d0 * d0
        for c in range(1, n_col):
            dc = cbuf[c] - mu
            tot = tot + dc * dc
        rstd = lax.rsqrt(jnp.sum(tot, axis=-1, keepdims=True) / e + EPS)
        for c in range(n_col):
            cbuf[c] = _silu((cbuf[c] - mu) * rstd * lg_ref[c] + lb_ref[c])
    for c in range(n_col):
        sl = slice(c * LANES, (c + 1) * LANES)
        for i in range(2):
            y = cbuf[c, pl.ds(i, half, stride=2), :]
            rows = slice(i * half, (i + 1) * half)
            o_ref[rows, sl] = (y * aux_ref[rows, sl].astype(F32)).astype(BF16)


def _cols(v, n_col):
    lead = v.shape[:-1]
    v = v.reshape(lead + (n_col, LANES))
    return jnp.moveaxis(v, -2, 0)


def _conv_mixer(u, aux, nb, seq, conv_w, conformer, conv_b=None, ln_g=None, ln_b=None):
    t, e = u.shape
    taps = conv_w.shape[0]
    n_col = e // LANES
    ts = 256
    ns = seq // ts
    hb = ts // HALO
    n_halo = seq // HALO
    u3 = u.reshape(nb, seq, e)
    aux3 = aux.reshape(nb, seq, e)
    tile = pl.BlockSpec((None, ts, e), lambda b, s: (b, s, 0))
    prev = pl.BlockSpec((None, HALO, e), lambda b, s: (b, jnp.maximum(s * hb - 1, 0), 0))
    nxt = pl.BlockSpec((None, HALO, e), lambda b, s: (b, jnp.minimum((s + 1) * hb, n_halo - 1), 0))
    in_specs = [tile, prev, nxt, tile,
                pl.BlockSpec((n_col, taps, LANES), lambda b, s: (0, 0, 0))]
    args = [u3, u3, u3, aux3, _cols(conv_w, n_col)]
    if conformer:
        vec = pl.BlockSpec((n_col, 1, LANES), lambda b, s: (0, 0, 0))
        in_specs += [vec, vec, vec]
        args += [_cols(conv_b[None, :], n_col), _cols(ln_g[None, :], n_col), _cols(ln_b[None, :], n_col)]
    scratch = [pltpu.VMEM((n_col, ts // 2 + 2 * HALO, LANES), jnp.uint32),
               pltpu.VMEM((n_col, taps, BF16_ROWS, LANES), BF16),
               pltpu.VMEM((n_col, ts, LANES), F32)]
    out = pl.pallas_call(
        functools.partial(_conv_kernel, taps, conformer, ts, n_col),
        grid=(nb, ns),
        in_specs=in_specs,
        out_specs=tile,
        out_shape=jax.ShapeDtypeStruct((nb, seq, e), BF16),
        scratch_shapes=scratch,
        compiler_params=_params(2),
        name="conformer_conv" if conformer else "short_conv",
    )(*args)
    return out.reshape(t, e)


def _nt_dot(a, b):
    return lax.dot_general(a, b, (((1,), (1,)), ((), ())), preferred_element_type=F32)


def _ctx_attn_kernel(n_heads, q_ref, k_ref, v_ref, sz_ref, o_ref):
    for hh in range(n_heads):
        sl = slice(hh * HEAD_DIM, (hh + 1) * HEAD_DIM)
        s = _nt_dot(q_ref[:, sl], k_ref[:, sl])
        p = jnp.exp(s - jnp.max(s, axis=-1, keepdims=True))
        l = jnp.sum(p, axis=-1, keepdims=True)
        o = jnp.dot(p.astype(BF16), v_ref[:, sl], preferred_element_type=F32) / l
        o_ref[:, sl] = (o * sz_ref[:, sl].astype(F32)).astype(BF16)


def _ctx_attention(q, k, v, sz, nb, seq):
    t, e = q.shape
    heads_per_step = 8
    w = heads_per_step * HEAD_DIM
    blk = pl.BlockSpec((seq, w), lambda b, g: (b, g))
    return pl.pallas_call(
        functools.partial(_ctx_attn_kernel, heads_per_step),
        grid=(nb, e // w),
        in_specs=[blk] * 4,
        out_specs=blk,
        out_shape=jax.ShapeDtypeStruct((t, e), BF16),
        compiler_params=_params(2),
        name="ctx_attention",
    )(q, k, v, sz)


def _nbr_geometry(rows):
    n_blocks = rows // Q_ROWS
    classes = {}
    for name, m in (("first", 0), ("mid", 1), ("last", n_blocks - 1)):
        kr0 = int(np.clip(Q_ROWS * m - WIN_H // 2, 0, rows - K_ROWS))
        per_row = []
        for i in range(Q_ROWS):
            r = Q_ROWS * m + i
            r0 = int(np.clip(r - WIN_H // 2, 0, rows - WIN_H))
            a = r0 - kr0
            assert 0 <= a and a + WIN_H <= K_ROWS
            jp0, jp1 = a // 2, (a + WIN_H - 1) // 2
            pairs = []
            for jp in range(jp0, jp1 + 1):
                halves = []
                for j in (2 * jp, 2 * jp + 1):
                    ok = a <= j < a + WIN_H
                    halves.append(kr0 + j - r + WIN_H - 1 if ok else None)
                pairs.append(tuple(halves))
            per_row.append((jp0, pairs))
        classes[name] = (kr0 - Q_ROWS * m, per_row)
    return classes


def _nbr_kernel(geom, tab_index, n_bias_rows, n_bias_cols,
                rpb_ref, q_ref, k_ref, v_ref, sz_ref, kc_ref, vc_ref, o_ref,
                half_l, half_r, tabs, s_loc, s_ctx, p_loc, p_ctx, linv, kcb, vcb):
    h = pl.program_id(0)
    b = pl.program_id(1)
    n_blocks = q_ref.shape[0] // (Q_ROWS * GRID_W)
    nq = Q_ROWS * GRID_W
    nk = K_ROWS * GRID_W

    @pl.when(b == 0)
    def _build_bias_tables():
        qc = lax.broadcasted_iota(jnp.int32, (GRID_W, 2 * GRID_W), 0)
        lane = lax.broadcasted_iota(jnp.int32, (GRID_W, 2 * GRID_W), 1)
        kc = lane & (GRID_W - 1)
        right = lane >= GRID_W
        cstart = jnp.clip(qc - WIN_W // 2, 0, GRID_W - WIN_W)
        col_ok = (kc >= cstart) & (kc < cstart + WIN_W)
        dcol = kc - qc + (WIN_W - 1)
        code_l = jnp.where(col_ok & jnp.logical_not(right), dcol, -1)
        code_r = jnp.where(col_ok & right, dcol, -1)
        base = h * (n_bias_rows * n_bias_cols)
        for d in range(n_bias_rows):
            acc_l = jnp.full((GRID_W, 2 * GRID_W), NEG_INF, F32)
            acc_r = acc_l
            for dc in range(n_bias_cols):
                val = rpb_ref[base + d * n_bias_cols + dc]
                acc_l = jnp.where(code_l == dc, val, acc_l)
                acc_r = jnp.where(code_r == dc, val, acc_r)
            half_l[d] = acc_l
            half_r[d] = acc_r
        for (dl, dr), t in tab_index.items():
            if dl is not None and dr is not None:
                tabs[t] = jnp.maximum(half_l[dl], half_r[dr])
            elif dl is not None:
                tabs[t] = half_l[dl]
            else:
                tabs[t] = half_r[dr]

    kcb[...] = kc_ref[...].astype(BF16)
    vcb[...] = vc_ref[...].astype(BF16)

    n_ctx = kcb.shape[0] // LANES

    def geometry(m):
        cls = "first" if m == 0 else ("last" if m == n_blocks - 1 else "mid")
        k_off, per_row = geom[cls]
        return m * nq, m * nq + k_off * GRID_W, per_row

    def scores(m):
        q0, k0, _ = geometry(m)
        q = q_ref[q0:q0 + nq, :]
        s_loc[m % 2] = _nt_dot(q, k_ref[k0:k0 + nk, :])
        s_ctx[m % 2] = _nt_dot(q, kcb[...])

    def softmax(m):
        _, _, per_row = geometry(m)
        sl_ref, sc_ref, pl_ref, pc_ref = s_loc.at[m % 2], s_ctx.at[m % 2], p_loc.at[m % 2], p_ctx.at[m % 2]
        for i in range(Q_ROWS):
            rs = slice(i * GRID_W, (i + 1) * GRID_W)
            jp0, pairs = per_row[i]
            lo, hi = jp0 * LANES, (jp0 + len(pairs)) * LANES

            def local(t):
                cs = slice(lo + t * LANES, lo + (t + 1) * LANES)
                return cs, sl_ref[rs, cs] + tabs[tab_index[pairs[t]]]

            def context(t):
                cs = slice(t * LANES, (t + 1) * LANES)
                return cs, sc_ref[rs, cs]

            vmax = context(0)[1]
            for t in range(1, n_ctx):
                vmax = jnp.maximum(vmax, context(t)[1])
            for t in range(len(pairs)):
                vmax = jnp.maximum(vmax, local(t)[1])
            mx = jnp.max(vmax, axis=-1, keepdims=True)
            vsum = jnp.zeros((GRID_W, LANES), F32)
            for t in range(n_ctx):
                cs, x = context(t)
                px = jnp.exp(x - mx)
                vsum = vsum + px
                pc_ref[rs, cs] = px.astype(BF16)
            for t in range(len(pairs)):
                cs, x = local(t)
                px = jnp.exp(x - mx)
                vsum = vsum + px
                pl_ref[rs, cs] = px.astype(BF16)
            if lo > 0:
                pl_ref[rs, 0:lo] = jnp.zeros((GRID_W, lo), BF16)
            if hi < nk:
                pl_ref[rs, hi:nk] = jnp.zeros((GRID_W, nk - hi), BF16)
            l = jnp.sum(vsum, axis=-1, keepdims=True)
            linv[m % 2, rs, :] = jnp.broadcast_to(1.0 / l, (GRID_W, HEAD_DIM))

    def values(m):
        q0, k0, _ = geometry(m)
        o = jnp.dot(p_loc[m % 2], v_ref[k0:k0 + nk, :], preferred_element_type=F32)
        o = o + jnp.dot(p_ctx[m % 2], vcb[...], preferred_element_type=F32)
        o = o * linv[m % 2] * sz_ref[q0:q0 + nq, :].astype(F32)
        o_ref[q0:q0 + nq, :] = o.astype(BF16)

    scores(0)
    for m in range(n_blocks):
        if m + 1 < n_blocks:
            scores(m + 1)
        softmax(m)
        values(m)


def _nbr_attention(q, k, v, sz, cache_k, cache_v, j, rpb):
    t, e = q.shape
    nb, _, past, n_heads, hd = cache_k.shape
    seq = t // nb
    rows = seq // GRID_W
    assert hd == HEAD_DIM and e == n_heads * HEAD_DIM and seq == rows * GRID_W
    assert rows % Q_ROWS == 0 and rows >= 3 * Q_ROWS and GRID_W * 2 == LANES
    n_bias_rows, n_bias_cols = rpb.shape[1], rpb.shape[2]
    assert n_bias_rows == 2 * WIN_H - 1 and n_bias_cols == 2 * WIN_W - 1
    geom = _nbr_geometry(rows)
    keys = sorted({pr for _, per_row in geom.values() for _, pairs in per_row for pr in pairs},
                  key=lambda pr: (pr[0] is None, pr[1] is None, pr))
    tab_index = {pr: n for n, pr in enumerate(keys)}
    nq, nk = Q_ROWS * GRID_W, K_ROWS * GRID_W
    head = pl.BlockSpec((seq, HEAD_DIM), lambda h, b: (b, h))
    ctx = pl.BlockSpec((None, None, past, HEAD_DIM), lambda h, b: (b, j, 0, h))
    ck = cache_k.reshape(nb, cache_k.shape[1], past, e)
    cv = cache_v.reshape(nb, cache_v.shape[1], past, e)
    return pl.pallas_call(
        functools.partial(_nbr_kernel, geom, tab_index, n_bias_rows, n_bias_cols),
        grid=(n_heads, nb),
        in_specs=[pl.BlockSpec(memory_space=pltpu.SMEM), head, head, head, head, ctx, ctx],
        out_specs=head,
        out_shape=jax.ShapeDtypeStruct((t, e), BF16),
        scratch_shapes=[pltpu.VMEM((n_bias_rows, GRID_W, LANES), F32),
                        pltpu.VMEM((n_bias_rows, GRID_W, LANES), F32),
                        pltpu.VMEM((len(keys), GRID_W, LANES), F32),
                        pltpu.VMEM((2, nq, nk), F32),
                        pltpu.VMEM((2, nq, past), F32),
                        pltpu.VMEM((2, nq, nk), BF16),
                        pltpu.VMEM((2, nq, past), BF16),
                        pltpu.VMEM((2, nq, HEAD_DIM), F32),
                        pltpu.VMEM((past, HEAD_DIM), BF16),
                        pltpu.VMEM((past, HEAD_DIM), BF16)],
        compiler_params=_params(2),
        name="nbr_attention",
    )(rpb.reshape(-1), q, k, v, sz, ck, cv)


def kernel(x_prompt, x_sample, cache_k, cache_v, c, c_ctx, w_ada, b_ada, g_pre, g_post,
           w_in_a, conv_w_a, conv_b_a, ln_g_a, ln_b_a, w_in_b, rpb_b, w_in_c, conv_w_c, w_out):
    depth = w_ada.shape[0]
    d = x_prompt.shape[-1]
    tm = 512
    tm_pre = 1024
    mods = _modulation(c, c_ctx, w_ada, b_ada)
    w_out_bf = w_out.astype(BF16)

    groups = []
    for x, is_ctx in ((x_prompt, True), (x_sample, False)):
        nb, seq, _ = x.shape
        if is_ctx:
            row_fn = lambda tile: (lambda i: 0)
        else:
            row_fn = functools.partial(lambda tile, per: (lambda i: 1 + i // (per // tile)), per=seq)
        x2 = x.reshape(nb * seq, d)
        groups.append(dict(x=x2, h=_prenorm(x2, mods, g_pre, 0, row_fn(tm_pre), tm_pre),
                           nb=nb, seq=seq, row_fn=row_fn(tm), is_ctx=is_ctx))

    new_k, new_v = [], []
    for i in range(depth):
        kind, j = i % N_MIXERS, i // N_MIXERS
        for grp in groups:
            nb, seq = grp["nb"], grp["seq"]
            if kind == 0:
                u, sz = _inproj(grp["h"], w_in_a, j, 0)
                g = _conv_mixer(u, sz, nb, seq, conv_w_a[j], True, conv_b_a[j], ln_g_a[j], ln_b_a[j])
            elif kind == 1:
                if grp["is_ctx"]:
                    q, k, v, sz, k32, v32 = _inproj(grp["h"], w_in_b, j, 1, emit_kv=True)
                    g = _ctx_attention(q, k, v, sz, nb, seq)
                    new_k.append(k32.reshape(nb, seq, -1, HEAD_DIM))
                    new_v.append(v32.reshape(nb, seq, -1, HEAD_DIM))
                else:
                    q, k, v, sz = _inproj(grp["h"], w_in_b, j, 1)
                    g = _nbr_attention(q, k, v, sz, cache_k, cache_v, j, rpb_b[j])
            else:
                cv, bz = _inproj(grp["h"], w_in_c, j, 2)
                g = _conv_mixer(cv, bz, nb, seq, conv_w_c[j], False)
            grp["x"], grp["h"] = _outproj(g, w_out_bf, i, grp["x"], mods, g_post, g_pre, grp["row_fn"], tm)

    y_prompt = groups[0]["x"].reshape(x_prompt.shape)
    y_sample = groups[1]["x"].reshape(x_sample.shape)
    return (y_prompt, y_sample, jnp.stack(new_k, axis=1), jnp.stack(new_v, axis=1))
```

```python
import functools

import numpy as np
import jax
import jax.numpy as jnp
from jax import lax
from jax.experimental import pallas as pl
from jax.experimental.pallas import tpu as pltpu

F32 = jnp.float32
BF16 = jnp.bfloat16

GRID_W = 64
HEAD_DIM = 128
WIN_H = 8
WIN_W = 16
N_MIXERS = 3
EPS = 1e-6
NEG_INF = -1e30
ATTN_SCALE = HEAD_DIM ** -0.5
LOG2_E = 1.4426950408889634

LANES = 128
BF16_ROWS = 16
HALO = BF16_ROWS
CONV_WORDS = 64
MOD_ROWS = 8
VMEM_LIMIT = 56 * 1024 * 1024
Q_ROWS = 8
K_ROWS = 16


def _params(n_axes):
    return pltpu.CompilerParams(dimension_semantics=("arbitrary",) * n_axes,
                                vmem_limit_bytes=VMEM_LIMIT)


def _sigmoid(x):
    return 1.0 / (1.0 + jnp.exp2(x * (-LOG2_E)))


def _silu(x):
    return x * _sigmoid(x)


def _mod_kernel(cond_ref, w_ref, b_ref, o_ref):
    s = _silu(cond_ref[...])
    o_ref[...] = jnp.dot(s, w_ref[...], preferred_element_type=F32,
                         precision=lax.Precision.HIGHEST) + b_ref[...]


def _modulation(c, c_ctx, w_ada, b_ada):
    depth, d, d3 = w_ada.shape
    pad = MOD_ROWS - 1 - c.shape[0]
    cond = jnp.concatenate([c_ctx[None, :], c, jnp.zeros((pad, d), F32)], axis=0)
    tn = 1024
    mods = pl.pallas_call(
        _mod_kernel,
        grid=(depth, d3 // tn),
        in_specs=[pl.BlockSpec((MOD_ROWS, d), lambda l, n: (0, 0)),
                  pl.BlockSpec((None, d, tn), lambda l, n: (l, 0, n)),
                  pl.BlockSpec((None, 1, tn), lambda l, n: (l, 0, n))],
        out_specs=pl.BlockSpec((None, MOD_ROWS, tn), lambda l, n: (l, 0, n)),
        out_shape=jax.ShapeDtypeStruct((depth, MOD_ROWS, d3), F32),
        compiler_params=_params(2),
        name="modulation",
    )(cond, w_ada, b_ada.reshape(depth, 1, d3))
    return mods.reshape(depth * MOD_ROWS, 1, d3)


def _mod_specs(layer, d, row_fn):
    def spec(col):
        return pl.BlockSpec((None, 1, d), lambda i: (layer * MOD_ROWS + row_fn(i), 0, col))
    return spec(0), spec(1), spec(2)


def _adaln(x, g, shift, scale):
    y = x * lax.rsqrt(jnp.mean(x * x, axis=-1, keepdims=True) + EPS)
    return (y * g) * (1.0 + scale) + shift


def _prenorm_kernel(x_ref, g_ref, sh_ref, sc_ref, h_ref):
    h_ref[...] = _adaln(x_ref[...], g_ref[...], sh_ref[...], sc_ref[...]).astype(BF16)


def _prenorm(x, mods, g_pre, layer, row_fn, tm):
    t, d = x.shape
    sh, sc, _ = _mod_specs(layer, d, row_fn)
    return pl.pallas_call(
        _prenorm_kernel,
        grid=(t // tm,),
        in_specs=[pl.BlockSpec((tm, d), lambda i: (i, 0)),
                  pl.BlockSpec((None, 1, d), lambda i: (layer, 0, 0)),
                  sh, sc],
        out_specs=pl.BlockSpec((tm, d), lambda i: (i, 0)),
        out_shape=jax.ShapeDtypeStruct((t, d), BF16),
        compiler_params=_params(1),
        name="prenorm",
    )(x, g_pre.reshape(g_pre.shape[0], 1, d), mods, mods)


def _inproj_kernel(kind, n_groups, tn, emit_kv, h_ref, *refs):
    w_refs = refs[:n_groups]
    out_refs = refs[n_groups:-1]
    wbf_ref = refs[-1]

    @pl.when(pl.program_id(1) == 0)
    def _cast_weights():
        for g in range(n_groups):
            wbf_ref[:, g * tn:(g + 1) * tn] = w_refs[g][...].astype(BF16)

    r = jnp.dot(h_ref[...], wbf_ref[...], preferred_element_type=F32)
    p = [r[:, g * tn:(g + 1) * tn] for g in range(n_groups)]
    if kind == 0:
        a, b, z = p
        out_refs[0][...] = (a * _sigmoid(b)).astype(BF16)
        out_refs[1][...] = _silu(z).astype(BF16)
    elif kind == 1:
        q, k, v, z = p
        out_refs[0][...] = (q * ATTN_SCALE).astype(BF16)
        out_refs[1][...] = k.astype(BF16)
        out_refs[2][...] = v.astype(BF16)
        out_refs[3][...] = _silu(z).astype(BF16)
        if emit_kv:
            out_refs[4][...] = k
            out_refs[5][...] = v
    else:
        bg, cg, v, z = p
        out_refs[0][...] = (cg * v).astype(BF16)
        out_refs[1][...] = (bg * _silu(z)).astype(BF16)


def _inproj(h, w_in, j, kind, emit_kv=False):
    t, d = h.shape
    n_groups = 3 if kind == 0 else 4
    e = w_in.shape[-1] // n_groups
    wide = kind == 0
    tm = min(1024 if (emit_kv or wide) else 2048, t)
    tn = 512 if wide else 256
    nj = e // tn
    w_specs = [pl.BlockSpec((None, d, tn), functools.partial(lambda jj, ii, g: (j, 0, g * nj + jj), g=g))
               for g in range(n_groups)]
    n_bf = {0: 2, 1: 4, 2: 2}[kind]
    out_shape = [jax.ShapeDtypeStruct((t, e), BF16)] * n_bf
    if emit_kv:
        out_shape = out_shape + [jax.ShapeDtypeStruct((t, e), F32)] * 2
    out_spec = pl.BlockSpec((tm, tn), lambda jj, ii: (ii, jj))
    return pl.pallas_call(
        functools.partial(_inproj_kernel, kind, n_groups, tn, emit_kv),
        grid=(nj, t // tm),
        in_specs=[pl.BlockSpec((tm, d), lambda jj, ii: (ii, 0))] + w_specs,
        out_specs=[out_spec] * len(out_shape),
        out_shape=out_shape,
        scratch_shapes=[pltpu.VMEM((d, n_groups * tn), BF16)],
        compiler_params=_params(2),
        name="inproj",
    )(h, *([w_in] * n_groups))


def _outproj_kernel(has_next, g_ref, w_ref, x_ref, gpost_ref, gate_ref, *refs):
    m = jnp.dot(g_ref[...], w_ref[...], preferred_element_type=F32)
    y = m * lax.rsqrt(jnp.mean(m * m, axis=-1, keepdims=True) + EPS)
    xn = x_ref[...] + gate_ref[...] * (y * gpost_ref[...])
    if has_next:
        gpre_ref, sh_ref, sc_ref, xo_ref, h_ref = refs
        h_ref[...] = _adaln(xn, gpre_ref[...], sh_ref[...], sc_ref[...]).astype(BF16)
    else:
        (xo_ref,) = refs
    xo_ref[...] = xn


def _outproj(g, w_out_bf, layer, x, mods, g_post, g_pre, row_fn, tm):
    t, e = g.shape
    d = x.shape[1]
    depth = w_out_bf.shape[0]
    has_next = layer + 1 < depth
    _, _, gate = _mod_specs(layer, d, row_fn)
    row = lambda i: (i, 0)
    in_specs = [pl.BlockSpec((tm, e), row),
                pl.BlockSpec((None, e, d), lambda i: (layer, 0, 0), pipeline_mode=pl.Buffered(1)),
                pl.BlockSpec((tm, d), row),
                pl.BlockSpec((None, 1, d), lambda i: (layer, 0, 0)),
                gate]
    args = [g, w_out_bf, x, g_post.reshape(depth, 1, d), mods]
    out_specs = [pl.BlockSpec((tm, d), row)]
    out_shape = [jax.ShapeDtypeStruct((t, d), F32)]
    if has_next:
        sh, sc, _ = _mod_specs(layer + 1, d, row_fn)
        in_specs += [pl.BlockSpec((None, 1, d), lambda i: (layer + 1, 0, 0)), sh, sc]
        args += [g_pre.reshape(depth, 1, d), mods, mods]
        out_specs.append(pl.BlockSpec((tm, d), row))
        out_shape.append(jax.ShapeDtypeStruct((t, d), BF16))
    outs = pl.pallas_call(
        functools.partial(_outproj_kernel, has_next),
        grid=(t // tm,),
        in_specs=in_specs,
        out_specs=out_specs,
        out_shape=out_shape,
        compiler_params=_params(1),
        name="outproj",
    )(*args)
    return (outs[0], outs[1]) if has_next else (outs[0], None)


def _conv_kernel(taps, conformer, ts, n_col, u_ref, up_ref, un_ref, aux_ref, w_ref, *refs):
    if conformer:
        cb_ref, lg_ref, lb_ref, o_ref, wpk, wb, cbuf = refs
    else:
        o_ref, wpk, wb, cbuf = refs
    s = pl.program_id(1)
    has_prev = s > 0
    has_next = s < pl.num_programs(1) - 1
    pad = taps // 2
    half = ts // 2

    @pl.when((pl.program_id(0) == 0) & (s == 0))
    def _broadcast_weights():
        def body(c, carry):
            for k in range(taps):
                wb[c, k] = jnp.broadcast_to(w_ref[c, k:k + 1, :], (BF16_ROWS, LANES)).astype(BF16)
            return carry
        lax.fori_loop(0, n_col, body, 0)

    def pack(lo, hi):
        return pltpu.pack_elementwise([lo.astype(F32), hi.astype(F32)], packed_dtype=BF16)

    for c in range(n_col):
        sl = slice(c * LANES, (c + 1) * LANES)
        prev = jnp.where(has_prev, up_ref[:, sl].astype(F32), 0.0)
        nxt = jnp.where(has_next, un_ref[:, sl].astype(F32), 0.0)
        wpk[c, 0:HALO, :] = pack(prev, u_ref[half - HALO:half, sl])
        wpk[c, HALO:HALO + half, :] = pack(u_ref[0:half, sl], u_ref[half:ts, sl])
        wpk[c, HALO + half:2 * HALO + half, :] = pack(u_ref[half:half + HALO, sl], nxt)

    def col_body(c, carry):
        def chunk_body(j, carry2):
            w0 = pl.multiple_of(j * CONV_WORDS, CONV_WORDS)
            acc = None
            for k in range(taps):
                x = pltpu.bitcast(wpk[c, pl.ds(w0 + (HALO - pad + k), CONV_WORDS), :], BF16)
                wv = jnp.concatenate([wb[c, k]] * (2 * CONV_WORDS // BF16_ROWS), axis=0)
                term = x.astype(F32) * wv.astype(F32)
                acc = term if acc is None else acc + term
            if conformer:
                acc = acc + cb_ref[c]
            cbuf[c, pl.ds(pl.multiple_of(2 * w0, 2 * CONV_WORDS), 2 * CONV_WORDS), :] = acc
            return carry2
        lax.fori_loop(0, half // CONV_WORDS, chunk_body, 0)
        return carry

    lax.fori_loop(0, n_col, col_body, 0)

    if conformer:
        e = n_col * LANES
        tot = cbuf[0]
        for c in range(1, n_col):
            tot = tot + cbuf[c]
        mu = jnp.sum(tot, axis=-1, keepdims=True) / e
        d0 = cbuf[0] - mu
        tot = d0 * d0
        for c in range(1, n_col):
            dc = cbuf[c] - mu
            tot = tot + dc * dc
        rstd = lax.rsqrt(jnp.sum(tot, axis=-1, keepdims=True) / e + EPS)
        for c in range(n_col):
            cbuf[c] = _silu((cbuf[c] - mu) * rstd * lg_ref[c] + lb_ref[c])
    for c in range(n_col):
        sl = slice(c * LANES, (c + 1) * LANES)
        for i in range(2):
            y = cbuf[c, pl.ds(i, half, stride=2), :]
            rows = slice(i * half, (i + 1) * half)
            o_ref[rows, sl] = (y * aux_ref[rows, sl].astype(F32)).astype(BF16)


def _cols(v, n_col):
    lead = v.shape[:-1]
    v = v.reshape(lead + (n_col, LANES))
    return jnp.moveaxis(v, -2, 0)


def _conv_mixer(u, aux, nb, seq, conv_w, conformer, conv_b=None, ln_g=None, ln_b=None):
    t, e = u.shape
    taps = conv_w.shape[0]
    n_col = e // LANES
    ts = 256
    ns = seq // ts
    hb = ts // HALO
    n_halo = seq // HALO
    u3 = u.reshape(nb, seq, e)
    aux3 = aux.reshape(nb, seq, e)
    tile = pl.BlockSpec((None, ts, e), lambda b, s: (b, s, 0))
    prev = pl.BlockSpec((None, HALO, e), lambda b, s: (b, jnp.maximum(s * hb - 1, 0), 0))
    nxt = pl.BlockSpec((None, HALO, e), lambda b, s: (b, jnp.minimum((s + 1) * hb, n_halo - 1), 0))
    in_specs = [tile, prev, nxt, tile,
                pl.BlockSpec((n_col, taps, LANES), lambda b, s: (0, 0, 0))]
    args = [u3, u3, u3, aux3, _cols(conv_w, n_col)]
    if conformer:
        vec = pl.BlockSpec((n_col, 1, LANES), lambda b, s: (0, 0, 0))
        in_specs += [vec, vec, vec]
        args += [_cols(conv_b[None, :], n_col), _cols(ln_g[None, :], n_col), _cols(ln_b[None, :], n_col)]
    scratch = [pltpu.VMEM((n_col, ts // 2 + 2 * HALO, LANES), jnp.uint32),
               pltpu.VMEM((n_col, taps, BF16_ROWS, LANES), BF16),
               pltpu.VMEM((n_col, ts, LANES), F32)]
    out = pl.pallas_call(
        functools.partial(_conv_kernel, taps, conformer, ts, n_col),
        grid=(nb, ns),
        in_specs=in_specs,
        out_specs=tile,
        out_shape=jax.ShapeDtypeStruct((nb, seq, e), BF16),
        scratch_shapes=scratch,
        compiler_params=_params(2),
        name="conformer_conv" if conformer else "short_conv",
    )(*args)
    return out.reshape(t, e)


def _nt_dot(a, b):
    return lax.dot_general(a, b, (((1,), (1,)), ((), ())), preferred_element_type=F32)


def _ctx_attn_kernel(n_heads, q_ref, k_ref, v_ref, sz_ref, o_ref):
    for hh in range(n_heads):
        sl = slice(hh * HEAD_DIM, (hh + 1) * HEAD_DIM)
        s = _nt_dot(q_ref[:, sl], k_ref[:, sl])
        p = jnp.exp(s - jnp.max(s, axis=-1, keepdims=True))
        l = jnp.sum(p, axis=-1, keepdims=True)
        o = jnp.dot(p.astype(BF16), v_ref[:, sl], preferred_element_type=F32) / l
        o_ref[:, sl] = (o * sz_ref[:, sl].astype(F32)).astype(BF16)


def _ctx_attention(q, k, v, sz, nb, seq):
    t, e = q.shape
    heads_per_step = 16
    w = heads_per_step * HEAD_DIM
    blk = pl.BlockSpec((seq, w), lambda b, g: (b, g))
    return pl.pallas_call(
        functools.partial(_ctx_attn_kernel, heads_per_step),
        grid=(nb, e // w),
        in_specs=[blk] * 4,
        out_specs=blk,
        out_shape=jax.ShapeDtypeStruct((t, e), BF16),
        compiler_params=_params(2),
        name="ctx_attention",
    )(q, k, v, sz)


def _nbr_geometry(rows):
    n_blocks = rows // Q_ROWS
    classes = {}
    for name, m in (("first", 0), ("mid", 1), ("last", n_blocks - 1)):
        kr0 = int(np.clip(Q_ROWS * m - WIN_H // 2, 0, rows - K_ROWS))
        per_row = []
        for i in range(Q_ROWS):
            r = Q_ROWS * m + i
            r0 = int(np.clip(r - WIN_H // 2, 0, rows - WIN_H))
            a = r0 - kr0
            assert 0 <= a and a + WIN_H <= K_ROWS
            jp0, jp1 = a // 2, (a + WIN_H - 1) // 2
            pairs = []
            for jp in range(jp0, jp1 + 1):
                halves = []
                for j in (2 * jp, 2 * jp + 1):
                    ok = a <= j < a + WIN_H
                    halves.append(kr0 + j - r + WIN_H - 1 if ok else None)
                pairs.append(tuple(halves))
            per_row.append((jp0, pairs))
        classes[name] = (kr0 - Q_ROWS * m, per_row)
    return classes


def _nbr_kernel(geom, tab_index, n_bias_rows, n_bias_cols,
                rpb_ref, q_ref, k_ref, v_ref, sz_ref, kc_ref, vc_ref, o_ref,
                half_l, half_r, tabs, s_loc, s_ctx, p_loc, p_ctx, linv, kcb, vcb):
    h = pl.program_id(0)
    b = pl.program_id(1)
    n_blocks = q_ref.shape[0] // (Q_ROWS * GRID_W)
    nq = Q_ROWS * GRID_W
    nk = K_ROWS * GRID_W

    @pl.when(b == 0)
    def _build_bias_tables():
        qc = lax.broadcasted_iota(jnp.int32, (GRID_W, 2 * GRID_W), 0)
        lane = lax.broadcasted_iota(jnp.int32, (GRID_W, 2 * GRID_W), 1)
        kc = lane & (GRID_W - 1)
        right = lane >= GRID_W
        cstart = jnp.clip(qc - WIN_W // 2, 0, GRID_W - WIN_W)
        col_ok = (kc >= cstart) & (kc < cstart + WIN_W)
        dcol = kc - qc + (WIN_W - 1)
        code_l = jnp.where(col_ok & jnp.logical_not(right), dcol, -1)
        code_r = jnp.where(col_ok & right, dcol, -1)
        base = h * (n_bias_rows * n_bias_cols)
        for d in range(n_bias_rows):
            acc_l = jnp.full((GRID_W, 2 * GRID_W), NEG_INF, F32)
            acc_r = acc_l
            for dc in range(n_bias_cols):
                val = rpb_ref[base + d * n_bias_cols + dc]
                acc_l = jnp.where(code_l == dc, val, acc_l)
                acc_r = jnp.where(code_r == dc, val, acc_r)
            half_l[d] = acc_l
            half_r[d] = acc_r
        for (dl, dr), t in tab_index.items():
            if dl is not None and dr is not None:
                tabs[t] = jnp.maximum(half_l[dl], half_r[dr])
            elif dl is not None:
                tabs[t] = half_l[dl]
            else:
                tabs[t] = half_r[dr]

    kcb[...] = kc_ref[...].astype(BF16)
    vcb[...] = vc_ref[...].astype(BF16)

    n_ctx = kcb.shape[0] // LANES

    def geometry(m):
        cls = "first" if m == 0 else ("last" if m == n_blocks - 1 else "mid")
        k_off, per_row = geom[cls]
        return m * nq, m * nq + k_off * GRID_W, per_row

    def scores(m):
        q0, k0, _ = geometry(m)
        q = q_ref[q0:q0 + nq, :]
        s_loc[m % 2] = _nt_dot(q, k_ref[k0:k0 + nk, :])
        s_ctx[m % 2] = _nt_dot(q, kcb[...])

    def softmax(m):
        _, _, per_row = geometry(m)
        sl_ref, sc_ref, pl_ref, pc_ref = s_loc.at[m % 2], s_ctx.at[m % 2], p_loc.at[m % 2], p_ctx.at[m % 2]
        for i in range(Q_ROWS):
            rs = slice(i * GRID_W, (i + 1) * GRID_W)
            jp0, pairs = per_row[i]
            lo, hi = jp0 * LANES, (jp0 + len(pairs)) * LANES

            def local(t):
                cs = slice(lo + t * LANES, lo + (t + 1) * LANES)
                return cs, sl_ref[rs, cs] + tabs[tab_index[pairs[t]]]

            def context(t):
                cs = slice(t * LANES, (t + 1) * LANES)
                return cs, sc_ref[rs, cs]

            vmax = context(0)[1]
            for t in range(1, n_ctx):
                vmax = jnp.maximum(vmax, context(t)[1])
            for t in range(len(pairs)):
                vmax = jnp.maximum(vmax, local(t)[1])
            mx = jnp.max(vmax, axis=-1, keepdims=True)
            vsum = jnp.zeros((GRID_W, LANES), F32)
            for t in range(n_ctx):
                cs, x = context(t)
                px = jnp.exp(x - mx)
                vsum = vsum + px
                pc_ref[rs, cs] = px.astype(BF16)
            for t in range(len(pairs)):
                cs, x = local(t)
                px = jnp.exp(x - mx)
                vsum = vsum + px
                pl_ref[rs, cs] = px.astype(BF16)
            if lo > 0:
                pl_ref[rs, 0:lo] = jnp.zeros((GRID_W, lo), BF16)
            if hi < nk:
                pl_ref[rs, hi:nk] = jnp.zeros((GRID_W, nk - hi), BF16)
            l = jnp.sum(vsum, axis=-1, keepdims=True)
            linv[m % 2, rs, :] = jnp.broadcast_to(1.0 / l, (GRID_W, HEAD_DIM))

    def values(m):
        q0, k0, _ = geometry(m)
        o = jnp.dot(p_loc[m % 2], v_ref[k0:k0 + nk, :], preferred_element_type=F32)
        o = o + jnp.dot(p_ctx[m % 2], vcb[...], preferred_element_type=F32)
        o = o * linv[m % 2] * sz_ref[q0:q0 + nq, :].astype(F32)
        o_ref[q0:q0 + nq, :] = o.astype(BF16)

    scores(0)
    for m in range(n_blocks):
        if m + 1 < n_blocks:
            scores(m + 1)
        softmax(m)
        values(m)


def _nbr_attention(q, k, v, sz, cache_k, cache_v, j, rpb):
    t, e = q.shape
    nb, _, past, n_heads, hd = cache_k.shape
    seq = t // nb
    rows = seq // GRID_W
    assert hd == HEAD_DIM and e == n_heads * HEAD_DIM and seq == rows * GRID_W
    assert rows % Q_ROWS == 0 and rows >= 3 * Q_ROWS and GRID_W * 2 == LANES
    n_bias_rows, n_bias_cols = rpb.shape[1], rpb.shape[2]
    assert n_bias_rows == 2 * WIN_H - 1 and n_bias_cols == 2 * WIN_W - 1
    geom = _nbr_geometry(rows)
    keys = sorted({pr for _, per_row in geom.values() for _, pairs in per_row for pr in pairs},
                  key=lambda pr: (pr[0] is None, pr[1] is None, pr))
    tab_index = {pr: n for n, pr in enumerate(keys)}
    nq, nk = Q_ROWS * GRID_W, K_ROWS * GRID_W
    head = pl.BlockSpec((seq, HEAD_DIM), lambda h, b: (b, h))
    ctx = pl.BlockSpec((None, None, past, HEAD_DIM), lambda h, b: (b, j, 0, h))
    ck = cache_k.reshape(nb, cache_k.shape[1], past, e)
    cv = cache_v.reshape(nb, cache_v.shape[1], past, e)
    return pl.pallas_call(
        functools.partial(_nbr_kernel, geom, tab_index, n_bias_rows, n_bias_cols),
        grid=(n_heads, nb),
        in_specs=[pl.BlockSpec(memory_space=pltpu.SMEM), head, head, head, head, ctx, ctx],
        out_specs=head,
        out_shape=jax.ShapeDtypeStruct((t, e), BF16),
        scratch_shapes=[pltpu.VMEM((n_bias_rows, GRID_W, LANES), F32),
                        pltpu.VMEM((n_bias_rows, GRID_W, LANES), F32),
                        pltpu.VMEM((len(keys), GRID_W, LANES), F32),
                        pltpu.VMEM((2, nq, nk), F32),
                        pltpu.VMEM((2, nq, past), F32),
                        pltpu.VMEM((2, nq, nk), BF16),
                        pltpu.VMEM((2, nq, past), BF16),
                        pltpu.VMEM((2, nq, HEAD_DIM), F32),
                        pltpu.VMEM((past, HEAD_DIM), BF16),
                        pltpu.VMEM((past, HEAD_DIM), BF16)],
        compiler_params=_params(2),
        name="nbr_attention",
    )(rpb.reshape(-1), q, k, v, sz, ck, cv)


def kernel(x_prompt, x_sample, cache_k, cache_v, c, c_ctx, w_ada, b_ada, g_pre, g_post,
           w_in_a, conv_w_a, conv_b_a, ln_g_a, ln_b_a, w_in_b, rpb_b, w_in_c, conv_w_c, w_out):
    depth = w_ada.shape[0]
    d = x_prompt.shape[-1]
    tm = 512
    tm_pre = 1024
    mods = _modulation(c, c_ctx, w_ada, b_ada)
    w_out_bf = w_out.astype(BF16)

    groups = []
    for x, is_ctx in ((x_prompt, True), (x_sample, False)):
        nb, seq, _ = x.shape
        if is_ctx:
            row_fn = lambda tile: (lambda i: 0)
        else:
            row_fn = functools.partial(lambda tile, per: (lambda i: 1 + i // (per // tile)), per=seq)
        x2 = x.reshape(nb * seq, d)
        groups.append(dict(x=x2, h=_prenorm(x2, mods, g_pre, 0, row_fn(tm_pre), tm_pre),
                           nb=nb, seq=seq, row_fn=row_fn(tm), is_ctx=is_ctx))

    new_k, new_v = [], []
    for i in range(depth):
        kind, j = i % N_MIXERS, i // N_MIXERS
        for grp in groups:
            nb, seq = grp["nb"], grp["seq"]
            if kind == 0:
                u, sz = _inproj(grp["h"], w_in_a, j, 0)
                g = _conv_mixer(u, sz, nb, seq, conv_w_a[j], True, conv_b_a[j], ln_g_a[j], ln_b_a[j])
            elif kind == 1:
                if grp["is_ctx"]:
                    q, k, v, sz, k32, v32 = _inproj(grp["h"], w_in_b, j, 1, emit_kv=True)
                    g = _ctx_attention(q, k, v, sz, nb, seq)
                    new_k.append(k32.reshape(nb, seq, -1, HEAD_DIM))
                    new_v.append(v32.reshape(nb, seq, -1, HEAD_DIM))
                else:
                    q, k, v, sz = _inproj(grp["h"], w_in_b, j, 1)
                    g = _nbr_attention(q, k, v, sz, cache_k, cache_v, j, rpb_b[j])
            else:
                cv, bz = _inproj(grp["h"], w_in_c, j, 2)
                g = _conv_mixer(cv, bz, nb, seq, conv_w_c[j], False)
            grp["x"], grp["h"] = _outproj(g, w_out_bf, i, grp["x"], mods, g_post, g_pre, grp["row_fn"], tm)

    y_prompt = groups[0]["x"].reshape(x_prompt.shape)
    y_sample = groups[1]["x"].reshape(x_sample.shape)
    return (y_prompt, y_sample, jnp.stack(new_k, axis=1), jnp.stack(new_v, axis=1))
```

```python
import functools

import numpy as np
import jax
import jax.numpy as jnp
from jax import lax
from jax.experimental import pallas as pl
from jax.experimental.pallas import tpu as pltpu

F32 = jnp.float32
BF16 = jnp.bfloat16

GRID_W = 64
HEAD_DIM = 128
WIN_H = 8
WIN_W = 16
N_MIXERS = 3
EPS = 1e-6
NEG_INF = -1e30
ATTN_SCALE = HEAD_DIM ** -0.5
LOG2_E = 1.4426950408889634

LANES = 128
BF16_ROWS = 16
HALO = BF16_ROWS
CONV_WORDS = 64
MOD_ROWS = 8
VMEM_LIMIT = 56 * 1024 * 1024
Q_ROWS = 8
K_ROWS = 16


def _params(n_axes):
    return pltpu.CompilerParams(dimension_semantics=("arbitrary",) * n_axes,
                                vmem_limit_bytes=VMEM_LIMIT)


def _sigmoid(x):
    return 1.0 / (1.0 + jnp.exp2(x * (-LOG2_E)))


def _silu(x):
    return x * _sigmoid(x)


def _mod_kernel(cond_ref, w_ref, b_ref, o_ref):
    s = _silu(cond_ref[...])
    o_ref[...] = jnp.dot(s, w_ref[...], preferred_element_type=F32,
                         precision=lax.Precision.HIGHEST) + b_ref[...]


def _modulation(c, c_ctx, w_ada, b_ada):
    depth, d, d3 = w_ada.shape
    pad = MOD_ROWS - 1 - c.shape[0]
    cond = jnp.concatenate([c_ctx[None, :], c, jnp.zeros((pad, d), F32)], axis=0)
    tn = 1024
    mods = pl.pallas_call(
        _mod_kernel,
        grid=(depth, d3 // tn),
        in_specs=[pl.BlockSpec((MOD_ROWS, d), lambda l, n: (0, 0)),
                  pl.BlockSpec((None, d, tn), lambda l, n: (l, 0, n)),
                  pl.BlockSpec((None, 1, tn), lambda l, n: (l, 0, n))],
        out_specs=pl.BlockSpec((None, MOD_ROWS, tn), lambda l, n: (l, 0, n)),
        out_shape=jax.ShapeDtypeStruct((depth, MOD_ROWS, d3), F32),
        compiler_params=_params(2),
        name="modulation",
    )(cond, w_ada, b_ada.reshape(depth, 1, d3))
    return mods.reshape(depth * MOD_ROWS, 1, d3)


def _mod_specs(layer, d, row_fn):
    def spec(col):
        return pl.BlockSpec((None, 1, d), lambda i: (layer * MOD_ROWS + row_fn(i), 0, col))
    return spec(0), spec(1), spec(2)


def _adaln(x, g, shift, scale):
    y = x * lax.rsqrt(jnp.mean(x * x, axis=-1, keepdims=True) + EPS)
    return (y * g) * (1.0 + scale) + shift


def _split_rows(tm, cols, index_fn, n_parts=2):
    part = tm // n_parts
    return [pl.BlockSpec((part, cols), functools.partial(
        lambda *idx, p: (index_fn(*idx) * n_parts + p, 0), p=p)) for p in range(n_parts)]


def _prenorm_kernel(x0_ref, x1_ref, g_ref, sh_ref, sc_ref, h_ref):
    half = x0_ref.shape[0]
    for p, x_ref in enumerate((x0_ref, x1_ref)):
        h_ref[p * half:(p + 1) * half, :] = _adaln(
            x_ref[...], g_ref[...], sh_ref[...], sc_ref[...]).astype(BF16)


def _prenorm(x, mods, g_pre, layer, row_fn, tm):
    t, d = x.shape
    sh, sc, _ = _mod_specs(layer, d, row_fn)
    return pl.pallas_call(
        _prenorm_kernel,
        grid=(t // tm,),
        in_specs=_split_rows(tm, d, lambda i: i) + [
            pl.BlockSpec((None, 1, d), lambda i: (layer, 0, 0)), sh, sc],
        out_specs=pl.BlockSpec((tm, d), lambda i: (i, 0)),
        out_shape=jax.ShapeDtypeStruct((t, d), BF16),
        compiler_params=_params(1),
        name="prenorm",
    )(x, x, g_pre.reshape(g_pre.shape[0], 1, d), mods, mods)


def _inproj_kernel(kind, n_groups, tn, emit_kv, h0_ref, h1_ref, *refs):
    w_refs = refs[:n_groups]
    out_refs = refs[n_groups:-1]
    wbf_ref = refs[-1]

    @pl.when(pl.program_id(1) == 0)
    def _cast_weights():
        for g in range(n_groups):
            wbf_ref[:, g * tn:(g + 1) * tn] = w_refs[g][...].astype(BF16)

    half = h0_ref.shape[0]
    for part, h_ref in enumerate((h0_ref, h1_ref)):
        rows = slice(part * half, (part + 1) * half)
        r = jnp.dot(h_ref[...], wbf_ref[...], preferred_element_type=F32)
        p = [r[:, g * tn:(g + 1) * tn] for g in range(n_groups)]
        if kind == 0:
            a, b, z = p
            out_refs[0][rows, :] = (a * _sigmoid(b)).astype(BF16)
            out_refs[1][rows, :] = _silu(z).astype(BF16)
        elif kind == 1:
            q, k, v, z = p
            out_refs[0][rows, :] = (q * ATTN_SCALE).astype(BF16)
            out_refs[1][rows, :] = k.astype(BF16)
            out_refs[2][rows, :] = v.astype(BF16)
            out_refs[3][rows, :] = _silu(z).astype(BF16)
            if emit_kv:
                out_refs[4][rows, :] = k
                out_refs[5][rows, :] = v
        else:
            bg, cg, v, z = p
            out_refs[0][rows, :] = (cg * v).astype(BF16)
            out_refs[1][rows, :] = (bg * _silu(z)).astype(BF16)


def _inproj(h, w_in, j, kind, emit_kv=False):
    t, d = h.shape
    n_groups = 3 if kind == 0 else 4
    e = w_in.shape[-1] // n_groups
    wide = kind == 0
    tm = min(1024 if (emit_kv or wide) else 2048, t)
    tn = 512 if wide else 256
    nj = e // tn
    w_specs = [pl.BlockSpec((None, d, tn), functools.partial(lambda jj, ii, g: (j, 0, g * nj + jj), g=g))
               for g in range(n_groups)]
    n_bf = {0: 2, 1: 4, 2: 2}[kind]
    out_shape = [jax.ShapeDtypeStruct((t, e), BF16)] * n_bf
    if emit_kv:
        out_shape = out_shape + [jax.ShapeDtypeStruct((t, e), F32)] * 2
    out_spec = pl.BlockSpec((tm, tn), lambda jj, ii: (ii, jj))
    return pl.pallas_call(
        functools.partial(_inproj_kernel, kind, n_groups, tn, emit_kv),
        grid=(nj, t // tm),
        in_specs=_split_rows(tm, d, lambda jj, ii: ii) + w_specs,
        out_specs=[out_spec] * len(out_shape),
        out_shape=out_shape,
        scratch_shapes=[pltpu.VMEM((d, n_groups * tn), BF16)],
        compiler_params=_params(2),
        name="inproj",
    )(h, h, *([w_in] * n_groups))


def _outproj_kernel(has_next, g0_ref, g1_ref, w_ref, x0_ref, x1_ref, gpost_ref, gate_ref, *refs):
    if has_next:
        gpre_ref, sh_ref, sc_ref, xo_ref, h_ref = refs
    else:
        (xo_ref,) = refs
    half = g0_ref.shape[0]
    for part, (g_ref, x_ref) in enumerate(((g0_ref, x0_ref), (g1_ref, x1_ref))):
        rows = slice(part * half, (part + 1) * half)
        m = jnp.dot(g_ref[...], w_ref[...], preferred_element_type=F32)
        y = m * lax.rsqrt(jnp.mean(m * m, axis=-1, keepdims=True) + EPS)
        xn = x_ref[...] + gate_ref[...] * (y * gpost_ref[...])
        if has_next:
            h_ref[rows, :] = _adaln(xn, gpre_ref[...], sh_ref[...], sc_ref[...]).astype(BF16)
        xo_ref[rows, :] = xn


def _outproj(g, w_out_bf, layer, x, mods, g_post, g_pre, row_fn, tm):
    t, e = g.shape
    d = x.shape[1]
    depth = w_out_bf.shape[0]
    has_next = layer + 1 < depth
    _, _, gate = _mod_specs(layer, d, row_fn)
    row = lambda i: (i, 0)
    in_specs = (_split_rows(tm, e, lambda i: i)
                + [pl.BlockSpec((None, e, d), lambda i: (layer, 0, 0), pipeline_mode=pl.Buffered(1))]
                + _split_rows(tm, d, lambda i: i)
                + [pl.BlockSpec((None, 1, d), lambda i: (layer, 0, 0)), gate])
    args = [g, g, w_out_bf, x, x, g_post.reshape(depth, 1, d), mods]
    out_specs = [pl.BlockSpec((tm, d), row)]
    out_shape = [jax.ShapeDtypeStruct((t, d), F32)]
    if has_next:
        sh, sc, _ = _mod_specs(layer + 1, d, row_fn)
        in_specs += [pl.BlockSpec((None, 1, d), lambda i: (layer + 1, 0, 0)), sh, sc]
        args += [g_pre.reshape(depth, 1, d), mods, mods]
        out_specs.append(pl.BlockSpec((tm, d), row))
        out_shape.append(jax.ShapeDtypeStruct((t, d), BF16))
    outs = pl.pallas_call(
        functools.partial(_outproj_kernel, has_next),
        grid=(t // tm,),
        in_specs=in_specs,
        out_specs=out_specs,
        out_shape=out_shape,
        compiler_params=_params(1),
        name="outproj",
    )(*args)
    return (outs[0], outs[1]) if has_next else (outs[0], None)


def _conv_kernel(taps, conformer, ts, n_col, u_ref, up_ref, un_ref, aux_ref, w_ref, *refs):
    if conformer:
        cb_ref, lg_ref, lb_ref, o_ref, wpk, wb, cbuf = refs
    else:
        o_ref, wpk, wb, cbuf = refs
    s = pl.program_id(1)
    has_prev = s > 0
    has_next = s < pl.num_programs(1) - 1
    pad = taps // 2
    half = ts // 2

    @pl.when((pl.program_id(0) == 0) & (s == 0))
    def _broadcast_weights():
        def body(c, carry):
            for k in range(taps):
                wb[c, k] = jnp.broadcast_to(w_ref[c, k:k + 1, :], (BF16_ROWS, LANES)).astype(BF16)
            return carry
        lax.fori_loop(0, n_col, body, 0)

    def pack(lo, hi):
        return pltpu.pack_elementwise([lo.astype(F32), hi.astype(F32)], packed_dtype=BF16)

    for c in range(n_col):
        sl = slice(c * LANES, (c + 1) * LANES)
        prev = jnp.where(has_prev, up_ref[:, sl].astype(F32), 0.0)
        nxt = jnp.where(has_next, un_ref[:, sl].astype(F32), 0.0)
        wpk[c, 0:HALO, :] = pack(prev, u_ref[half - HALO:half, sl])
        wpk[c, HALO:HALO + half, :] = pack(u_ref[0:half, sl], u_ref[half:ts, sl])
        wpk[c, HALO + half:2 * HALO + half, :] = pack(u_ref[half:half + HALO, sl], nxt)

    def col_body(c, carry):
        def chunk_body(j, carry2):
            w0 = pl.multiple_of(j * CONV_WORDS, CONV_WORDS)
            acc = None
            for k in range(taps):
                x = pltpu.bitcast(wpk[c, pl.ds(w0 + (HALO - pad + k), CONV_WORDS), :], BF16)
                wv = jnp.concatenate([wb[c, k]] * (2 * CONV_WORDS // BF16_ROWS), axis=0)
                term = x.astype(F32) * wv.astype(F32)
                acc = term if acc is None else acc + term
            if conformer:
                acc = acc + cb_ref[c]
            cbuf[c, pl.ds(pl.multiple_of(2 * w0, 2 * CONV_WORDS), 2 * CONV_WORDS), :] = acc
            return carry2
        lax.fori_loop(0, half // CONV_WORDS, chunk_body, 0)
        return carry

    lax.fori_loop(0, n_col, col_body, 0)

    if conformer:
        e = n_col * LANES
        tot = cbuf[0]
        for c in range(1, n_col):
            tot = tot + cbuf[c]
        mu = jnp.sum(tot, axis=-1, keepdims=True) / e
        d0 = cbuf[0] - mu
        tot = d0 * d0
        for c in range(1, n_col):
            dc = cbuf[c] - mu
            tot = tot + dc * dc
        rstd = lax.rsqrt(jnp.sum(tot, axis=-1, keepdims=True) / e + EPS)
        for c in range(n_col):
            cbuf[c] = _silu((cbuf[c] - mu) * rstd * lg_ref[c] + lb_ref[c])
    for c in range(n_col):
        sl = slice(c * LANES, (c + 1) * LANES)
        for i in range(2):
            y = cbuf[c, pl.ds(i, half, stride=2), :]
            rows = slice(i * half, (i + 1) * half)
            o_ref[rows, sl] = (y * aux_ref[rows, sl].astype(F32)).astype(BF16)


def _cols(v, n_col):
    lead = v.shape[:-1]
    v = v.reshape(lead + (n_col, LANES))
    return jnp.moveaxis(v, -2, 0)


def _conv_mixer(u, aux, nb, seq, conv_w, conformer, conv_b=None, ln_g=None, ln_b=None):
    t, e = u.shape
    taps = conv_w.shape[0]
    n_col = e // LANES
    ts = 256
    ns = seq // ts
    hb = ts // HALO
    n_halo = seq // HALO
    u3 = u.reshape(nb, seq, e)
    aux3 = aux.reshape(nb, seq, e)
    tile = pl.BlockSpec((None, ts, e), lambda b, s: (b, s, 0))
    prev = pl.BlockSpec((None, HALO, e), lambda b, s: (b, jnp.maximum(s * hb - 1, 0), 0))
    nxt = pl.BlockSpec((None, HALO, e), lambda b, s: (b, jnp.minimum((s + 1) * hb, n_halo - 1), 0))
    in_specs = [tile, prev, nxt, tile,
                pl.BlockSpec((n_col, taps, LANES), lambda b, s: (0, 0, 0))]
    args = [u3, u3, u3, aux3, _cols(conv_w, n_col)]
    if conformer:
        vec = pl.BlockSpec((n_col, 1, LANES), lambda b, s: (0, 0, 0))
        in_specs += [vec, vec, vec]
        args += [_cols(conv_b[None, :], n_col), _cols(ln_g[None, :], n_col), _cols(ln_b[None, :], n_col)]
    scratch = [pltpu.VMEM((n_col, ts // 2 + 2 * HALO, LANES), jnp.uint32),
               pltpu.VMEM((n_col, taps, BF16_ROWS, LANES), BF16),
               pltpu.VMEM((n_col, ts, LANES), F32)]
    out = pl.pallas_call(
        functools.partial(_conv_kernel, taps, conformer, ts, n_col),
        grid=(nb, ns),
        in_specs=in_specs,
        out_specs=tile,
        out_shape=jax.ShapeDtypeStruct((nb, seq, e), BF16),
        scratch_shapes=scratch,
        compiler_params=_params(2),
        name="conformer_conv" if conformer else "short_conv",
    )(*args)
    return out.reshape(t, e)


def _nt_dot(a, b):
    return lax.dot_general(a, b, (((1,), (1,)), ((), ())), preferred_element_type=F32)


def _ctx_attn_kernel(n_heads, q_ref, k_ref, v_ref, sz_ref, o_ref):
    for hh in range(n_heads):
        sl = slice(hh * HEAD_DIM, (hh + 1) * HEAD_DIM)
        s = _nt_dot(q_ref[:, sl], k_ref[:, sl])
        p = jnp.exp(s - jnp.max(s, axis=-1, keepdims=True))
        l = jnp.sum(p, axis=-1, keepdims=True)
        o = jnp.dot(p.astype(BF16), v_ref[:, sl], preferred_element_type=F32) / l
        o_ref[:, sl] = (o * sz_ref[:, sl].astype(F32)).astype(BF16)


def _ctx_attention(q, k, v, sz, nb, seq):
    t, e = q.shape
    heads_per_step = 16
    w = heads_per_step * HEAD_DIM
    blk = pl.BlockSpec((seq, w), lambda b, g: (b, g))
    return pl.pallas_call(
        functools.partial(_ctx_attn_kernel, heads_per_step),
        grid=(nb, e // w),
        in_specs=[blk] * 4,
        out_specs=blk,
        out_shape=jax.ShapeDtypeStruct((t, e), BF16),
        compiler_params=_params(2),
        name="ctx_attention",
    )(q, k, v, sz)


def _nbr_geometry(rows):
    n_blocks = rows // Q_ROWS
    classes = {}
    for name, m in (("first", 0), ("mid", 1), ("last", n_blocks - 1)):
        kr0 = int(np.clip(Q_ROWS * m - WIN_H // 2, 0, rows - K_ROWS))
        per_row = []
        for i in range(Q_ROWS):
            r = Q_ROWS * m + i
            r0 = int(np.clip(r - WIN_H // 2, 0, rows - WIN_H))
            a = r0 - kr0
            assert 0 <= a and a + WIN_H <= K_ROWS
            jp0, jp1 = a // 2, (a + WIN_H - 1) // 2
            pairs = []
            for jp in range(jp0, jp1 + 1):
                halves = []
                for j in (2 * jp, 2 * jp + 1):
                    ok = a <= j < a + WIN_H
                    halves.append(kr0 + j - r + WIN_H - 1 if ok else None)
                pairs.append(tuple(halves))
            per_row.append((jp0, pairs))
        classes[name] = (kr0 - Q_ROWS * m, per_row)
    return classes


def _nbr_kernel(geom, tab_index, n_bias_rows, n_bias_cols,
                rpb_ref, q_ref, k_ref, v_ref, sz_ref, kc_ref, vc_ref, o_ref,
                half_l, half_r, tabs, s_loc, s_ctx, p_loc, p_ctx, linv, kcb, vcb):
    h = pl.program_id(0)
    b = pl.program_id(1)
    n_blocks = q_ref.shape[0] // (Q_ROWS * GRID_W)
    nq = Q_ROWS * GRID_W
    nk = K_ROWS * GRID_W

    @pl.when(b == 0)
    def _build_bias_tables():
        qc = lax.broadcasted_iota(jnp.int32, (GRID_W, 2 * GRID_W), 0)
        lane = lax.broadcasted_iota(jnp.int32, (GRID_W, 2 * GRID_W), 1)
        kc = lane & (GRID_W - 1)
        right = lane >= GRID_W
        cstart = jnp.clip(qc - WIN_W // 2, 0, GRID_W - WIN_W)
        col_ok = (kc >= cstart) & (kc < cstart + WIN_W)
        dcol = kc - qc + (WIN_W - 1)
        code_l = jnp.where(col_ok & jnp.logical_not(right), dcol, -1)
        code_r = jnp.where(col_ok & right, dcol, -1)
        base = h * (n_bias_rows * n_bias_cols)
        for d in range(n_bias_rows):
            acc_l = jnp.full((GRID_W, 2 * GRID_W), NEG_INF, F32)
            acc_r = acc_l
            for dc in range(n_bias_cols):
                val = rpb_ref[base + d * n_bias_cols + dc]
                acc_l = jnp.where(code_l == dc, val, acc_l)
                acc_r = jnp.where(code_r == dc, val, acc_r)
            half_l[d] = acc_l
            half_r[d] = acc_r
        for (dl, dr), t in tab_index.items():
            if dl is not None and dr is not None:
                tabs[t] = jnp.maximum(half_l[dl], half_r[dr])
            elif dl is not None:
                tabs[t] = half_l[dl]
            else:
                tabs[t] = half_r[dr]

    kcb[...] = kc_ref[...].astype(BF16)
    vcb[...] = vc_ref[...].astype(BF16)

    n_ctx = kcb.shape[0] // LANES

    def geometry(m):
        cls = "first" if m == 0 else ("last" if m == n_blocks - 1 else "mid")
        k_off, per_row = geom[cls]
        return m * nq, m * nq + k_off * GRID_W, per_row

    def scores(m):
        q0, k0, _ = geometry(m)
        q = q_ref[q0:q0 + nq, :]
        s_loc[m % 2] = _nt_dot(q, k_ref[k0:k0 + nk, :])
        s_ctx[m % 2] = _nt_dot(q, kcb[...])

    def softmax(m):
        _, _, per_row = geometry(m)
        sl_ref, sc_ref, pl_ref, pc_ref = s_loc.at[m % 2], s_ctx.at[m % 2], p_loc.at[m % 2], p_ctx.at[m % 2]
        for i in range(Q_ROWS):
            rs = slice(i * GRID_W, (i + 1) * GRID_W)
            jp0, pairs = per_row[i]
            lo, hi = jp0 * LANES, (jp0 + len(pairs)) * LANES

            def local(t):
                cs = slice(lo + t * LANES, lo + (t + 1) * LANES)
                return cs, sl_ref[rs, cs] + tabs[tab_index[pairs[t]]]

            def context(t):
                cs = slice(t * LANES, (t + 1) * LANES)
                return cs, sc_ref[rs, cs]

            vmax = context(0)[1]
            for t in range(1, n_ctx):
                vmax = jnp.maximum(vmax, context(t)[1])
            for t in range(len(pairs)):
                vmax = jnp.maximum(vmax, local(t)[1])
            mx = jnp.max(vmax, axis=-1, keepdims=True)
            vsum = jnp.zeros((GRID_W, LANES), F32)
            for t in range(n_ctx):
                cs, x = context(t)
                px = jnp.exp(x - mx)
                vsum = vsum + px
                pc_ref[rs, cs] = px.astype(BF16)
            for t in range(len(pairs)):
                cs, x = local(t)
                px = jnp.exp(x - mx)
                vsum = vsum + px
                pl_ref[rs, cs] = px.astype(BF16)
            if lo > 0:
                pl_ref[rs, 0:lo] = jnp.zeros((GRID_W, lo), BF16)
            if hi < nk:
                pl_ref[rs, hi:nk] = jnp.zeros((GRID_W, nk - hi), BF16)
            l = jnp.sum(vsum, axis=-1, keepdims=True)
            linv[m % 2, rs, :] = jnp.broadcast_to(1.0 / l, (GRID_W, HEAD_DIM))

    def values(m):
        q0, k0, _ = geometry(m)
        o = jnp.dot(p_loc[m % 2], v_ref[k0:k0 + nk, :], preferred_element_type=F32)
        o = o + jnp.dot(p_ctx[m % 2], vcb[...], preferred_element_type=F32)
        o = o * linv[m % 2] * sz_ref[q0:q0 + nq, :].astype(F32)
        o_ref[q0:q0 + nq, :] = o.astype(BF16)

    scores(0)
    for m in range(n_blocks):
        if m + 1 < n_blocks:
            scores(m + 1)
        softmax(m)
        values(m)


def _nbr_attention(q, k, v, sz, cache_k, cache_v, j, rpb):
    t, e = q.shape
    nb, _, past, n_heads, hd = cache_k.shape
    seq = t // nb
    rows = seq // GRID_W
    assert hd == HEAD_DIM and e == n_heads * HEAD_DIM and seq == rows * GRID_W
    assert rows % Q_ROWS == 0 and rows >= 3 * Q_ROWS and GRID_W * 2 == LANES
    n_bias_rows, n_bias_cols = rpb.shape[1], rpb.shape[2]
    assert n_bias_rows == 2 * WIN_H - 1 and n_bias_cols == 2 * WIN_W - 1
    geom = _nbr_geometry(rows)
    keys = sorted({pr for _, per_row in geom.values() for _, pairs in per_row for pr in pairs},
                  key=lambda pr: (pr[0] is None, pr[1] is None, pr))
    tab_index = {pr: n for n, pr in enumerate(keys)}
    nq, nk = Q_ROWS * GRID_W, K_ROWS * GRID_W
    head = pl.BlockSpec((seq, HEAD_DIM), lambda h, b: (b, h))
    ctx = pl.BlockSpec((None, None, past, HEAD_DIM), lambda h, b: (b, j, 0, h))
    ck = cache_k.reshape(nb, cache_k.shape[1], past, e)
    cv = cache_v.reshape(nb, cache_v.shape[1], past, e)
    return pl.pallas_call(
        functools.partial(_nbr_kernel, geom, tab_index, n_bias_rows, n_bias_cols),
        grid=(n_heads, nb),
        in_specs=[pl.BlockSpec(memory_space=pltpu.SMEM), head, head, head, head, ctx, ctx],
        out_specs=head,
        out_shape=jax.ShapeDtypeStruct((t, e), BF16),
        scratch_shapes=[pltpu.VMEM((n_bias_rows, GRID_W, LANES), F32),
                        pltpu.VMEM((n_bias_rows, GRID_W, LANES), F32),
                        pltpu.VMEM((len(keys), GRID_W, LANES), F32),
                        pltpu.VMEM((2, nq, nk), F32),
                        pltpu.VMEM((2, nq, past), F32),
                        pltpu.VMEM((2, nq, nk), BF16),
                        pltpu.VMEM((2, nq, past), BF16),
                        pltpu.VMEM((2, nq, HEAD_DIM), F32),
                        pltpu.VMEM((past, HEAD_DIM), BF16),
                        pltpu.VMEM((past, HEAD_DIM), BF16)],
        compiler_params=_params(2),
        name="nbr_attention",
    )(rpb.reshape(-1), q, k, v, sz, ck, cv)


def kernel(x_prompt, x_sample, cache_k, cache_v, c, c_ctx, w_ada, b_ada, g_pre, g_post,
           w_in_a, conv_w_a, conv_b_a, ln_g_a, ln_b_a, w_in_b, rpb_b, w_in_c, conv_w_c, w_out):
    depth = w_ada.shape[0]
    d = x_prompt.shape[-1]
    tm = 512
    tm_pre = 1024
    mods = _modulation(c, c_ctx, w_ada, b_ada)
    w_out_bf = w_out.astype(BF16)

    groups = []
    for x, is_ctx in ((x_prompt, True), (x_sample, False)):
        nb, seq, _ = x.shape
        if is_ctx:
            row_fn = lambda tile: (lambda i: 0)
        else:
            row_fn = functools.partial(lambda tile, per: (lambda i: 1 + i // (per // tile)), per=seq)
        x2 = x.reshape(nb * seq, d)
        groups.append(dict(x=x2, h=_prenorm(x2, mods, g_pre, 0, row_fn(tm_pre), tm_pre),
                           nb=nb, seq=seq, row_fn=row_fn(tm), is_ctx=is_ctx))

    new_k, new_v = [], []
    for i in range(depth):
        kind, j = i % N_MIXERS, i // N_MIXERS
        for grp in groups:
            nb, seq = grp["nb"], grp["seq"]
            if kind == 0:
                u, sz = _inproj(grp["h"], w_in_a, j, 0)
                g = _conv_mixer(u, sz, nb, seq, conv_w_a[j], True, conv_b_a[j], ln_g_a[j], ln_b_a[j])
            elif kind == 1:
                if grp["is_ctx"]:
                    q, k, v, sz, k32, v32 = _inproj(grp["h"], w_in_b, j, 1, emit_kv=True)
                    g = _ctx_attention(q, k, v, sz, nb, seq)
                    new_k.append(k32.reshape(nb, seq, -1, HEAD_DIM))
                    new_v.append(v32.reshape(nb, seq, -1, HEAD_DIM))
                else:
                    q, k, v, sz = _inproj(grp["h"], w_in_b, j, 1)
                    g = _nbr_attention(q, k, v, sz, cache_k, cache_v, j, rpb_b[j])
            else:
                cv, bz = _inproj(grp["h"], w_in_c, j, 2)
                g = _conv_mixer(cv, bz, nb, seq, conv_w_c[j], False)
            grp["x"], grp["h"] = _outproj(g, w_out_bf, i, grp["x"], mods, g_post, g_pre, grp["row_fn"], tm)

    y_prompt = groups[0]["x"].reshape(x_prompt.shape)
    y_sample = groups[1]["x"].reshape(x_sample.shape)
    return (y_prompt, y_sample, jnp.stack(new_k, axis=1), jnp.stack(new_v, axis=1))
```

```python
import functools

import numpy as np
import jax
import jax.numpy as jnp
from jax import lax
from jax.experimental import pallas as pl
from jax.experimental.pallas import tpu as pltpu

F32 = jnp.float32
BF16 = jnp.bfloat16

GRID_W = 64
HEAD_DIM = 128
WIN_H = 8
WIN_W = 16
N_MIXERS = 3
EPS = 1e-6
NEG_INF = -1e30
ATTN_SCALE = HEAD_DIM ** -0.5
LOG2_E = 1.4426950408889634

LANES = 128
BF16_ROWS = 16
HALO = BF16_ROWS
CONV_WORDS = 64
MOD_ROWS = 8
VMEM_LIMIT = 56 * 1024 * 1024
IN_SLABS = 4
Q_ROWS = 8
K_ROWS = 16


def _params(n_axes):
    return pltpu.CompilerParams(dimension_semantics=("arbitrary",) * n_axes,
                                vmem_limit_bytes=VMEM_LIMIT)


def _sigmoid(x):
    return 1.0 / (1.0 + jnp.exp2(x * (-LOG2_E)))


def _silu(x):
    return x * _sigmoid(x)


def _mod_kernel(cond_ref, w_ref, b_ref, o_ref):
    s = _silu(cond_ref[...])
    o_ref[...] = jnp.dot(s, w_ref[...], preferred_element_type=F32,
                         precision=lax.Precision.HIGHEST) + b_ref[...]


def _modulation(c, c_ctx, w_ada, b_ada):
    depth, d, d3 = w_ada.shape
    pad = MOD_ROWS - 1 - c.shape[0]
    cond = jnp.concatenate([c_ctx[None, :], c, jnp.zeros((pad, d), F32)], axis=0)
    tn = 1024
    mods = pl.pallas_call(
        _mod_kernel,
        grid=(depth, d3 // tn),
        in_specs=[pl.BlockSpec((MOD_ROWS, d), lambda l, n: (0, 0)),
                  pl.BlockSpec((None, d, tn), lambda l, n: (l, 0, n)),
                  pl.BlockSpec((None, 1, tn), lambda l, n: (l, 0, n))],
        out_specs=pl.BlockSpec((None, MOD_ROWS, tn), lambda l, n: (l, 0, n)),
        out_shape=jax.ShapeDtypeStruct((depth, MOD_ROWS, d3), F32),
        compiler_params=_params(2),
        name="modulation",
    )(cond, w_ada, b_ada.reshape(depth, 1, d3))
    return mods.reshape(depth * MOD_ROWS, 1, d3)


def _mod_specs(layer, d, row_fn):
    def spec(col):
        return pl.BlockSpec((None, 1, d), lambda i: (layer * MOD_ROWS + row_fn(i), 0, col))
    return spec(0), spec(1), spec(2)


def _adaln(x, g, shift, scale):
    y = x * lax.rsqrt(jnp.mean(x * x, axis=-1, keepdims=True) + EPS)
    return (y * g) * (1.0 + scale) + shift


def _split_rows(tm, cols, index_fn, n_parts=2):
    part = tm // n_parts
    return [pl.BlockSpec((part, cols), functools.partial(
        lambda *idx, p: (index_fn(*idx) * n_parts + p, 0), p=p)) for p in range(n_parts)]


def _prenorm_kernel(n_parts, *refs):
    x_refs = refs[:n_parts]
    g_ref, sh_ref, sc_ref, h_ref = refs[n_parts:]
    rows = x_refs[0].shape[0]
    for p, x_ref in enumerate(x_refs):
        h_ref[p * rows:(p + 1) * rows, :] = _adaln(
            x_ref[...], g_ref[...], sh_ref[...], sc_ref[...]).astype(BF16)


def _prenorm(x, mods, g_pre, layer, row_fn, tm):
    t, d = x.shape
    sh, sc, _ = _mod_specs(layer, d, row_fn)
    return pl.pallas_call(
        functools.partial(_prenorm_kernel, IN_SLABS),
        grid=(t // tm,),
        in_specs=_split_rows(tm, d, lambda i: i, IN_SLABS) + [
            pl.BlockSpec((None, 1, d), lambda i: (layer, 0, 0)), sh, sc],
        out_specs=pl.BlockSpec((tm, d), lambda i: (i, 0)),
        out_shape=jax.ShapeDtypeStruct((t, d), BF16),
        compiler_params=_params(1),
        name="prenorm",
    )(*([x] * IN_SLABS), g_pre.reshape(g_pre.shape[0], 1, d), mods, mods)


def _inproj_kernel(kind, n_groups, tn, emit_kv, n_parts, *refs):
    h_refs = refs[:n_parts]
    refs = refs[n_parts:]
    w_refs = refs[:n_groups]
    out_refs = refs[n_groups:-1]
    wbf_ref = refs[-1]

    @pl.when(pl.program_id(1) == 0)
    def _cast_weights():
        for g in range(n_groups):
            wbf_ref[:, g * tn:(g + 1) * tn] = w_refs[g][...].astype(BF16)

    slab = h_refs[0].shape[0]
    for part, h_ref in enumerate(h_refs):
        rows = slice(part * slab, (part + 1) * slab)
        r = jnp.dot(h_ref[...], wbf_ref[...], preferred_element_type=F32)
        p = [r[:, g * tn:(g + 1) * tn] for g in range(n_groups)]
        if kind == 0:
            a, b, z = p
            out_refs[0][rows, :] = (a * _sigmoid(b)).astype(BF16)
            out_refs[1][rows, :] = _silu(z).astype(BF16)
        elif kind == 1:
            q, k, v, z = p
            out_refs[0][rows, :] = (q * ATTN_SCALE).astype(BF16)
            out_refs[1][rows, :] = k.astype(BF16)
            out_refs[2][rows, :] = v.astype(BF16)
            out_refs[3][rows, :] = _silu(z).astype(BF16)
            if emit_kv:
                out_refs[4][rows, :] = k
                out_refs[5][rows, :] = v
        else:
            bg, cg, v, z = p
            out_refs[0][rows, :] = (cg * v).astype(BF16)
            out_refs[1][rows, :] = (bg * _silu(z)).astype(BF16)


def _inproj(h, w_in, j, kind, emit_kv=False):
    t, d = h.shape
    n_groups = 3 if kind == 0 else 4
    e = w_in.shape[-1] // n_groups
    wide = kind == 0
    tm = min(1024 if (emit_kv or wide) else 2048, t)
    tn = 512 if wide else 256
    nj = e // tn
    w_specs = [pl.BlockSpec((None, d, tn), functools.partial(lambda jj, ii, g: (j, 0, g * nj + jj), g=g))
               for g in range(n_groups)]
    n_bf = {0: 2, 1: 4, 2: 2}[kind]
    out_shape = [jax.ShapeDtypeStruct((t, e), BF16)] * n_bf
    if emit_kv:
        out_shape = out_shape + [jax.ShapeDtypeStruct((t, e), F32)] * 2
    out_spec = pl.BlockSpec((tm, tn), lambda jj, ii: (ii, jj))
    return pl.pallas_call(
        functools.partial(_inproj_kernel, kind, n_groups, tn, emit_kv, IN_SLABS),
        grid=(nj, t // tm),
        in_specs=_split_rows(tm, d, lambda jj, ii: ii, IN_SLABS) + w_specs,
        out_specs=[out_spec] * len(out_shape),
        out_shape=out_shape,
        scratch_shapes=[pltpu.VMEM((d, n_groups * tn), BF16)],
        compiler_params=_params(2),
        name="inproj",
    )(*([h] * IN_SLABS), *([w_in] * n_groups))


def _outproj_kernel(has_next, g0_ref, g1_ref, w_ref, x0_ref, x1_ref, gpost_ref, gate_ref, *refs):
    if has_next:
        gpre_ref, sh_ref, sc_ref, xo_ref, h_ref = refs
    else:
        (xo_ref,) = refs
    half = g0_ref.shape[0]
    for part, (g_ref, x_ref) in enumerate(((g0_ref, x0_ref), (g1_ref, x1_ref))):
        rows = slice(part * half, (part + 1) * half)
        m = jnp.dot(g_ref[...], w_ref[...], preferred_element_type=F32)
        y = m * lax.rsqrt(jnp.mean(m * m, axis=-1, keepdims=True) + EPS)
        xn = x_ref[...] + gate_ref[...] * (y * gpost_ref[...])
        if has_next:
            h_ref[rows, :] = _adaln(xn, gpre_ref[...], sh_ref[...], sc_ref[...]).astype(BF16)
        xo_ref[rows, :] = xn


def _outproj(g, w_out_bf, layer, x, mods, g_post, g_pre, row_fn, tm):
    t, e = g.shape
    d = x.shape[1]
    depth = w_out_bf.shape[0]
    has_next = layer + 1 < depth
    _, _, gate = _mod_specs(layer, d, row_fn)
    row = lambda i: (i, 0)
    in_specs = (_split_rows(tm, e, lambda i: i)
                + [pl.BlockSpec((None, e, d), lambda i: (layer, 0, 0), pipeline_mode=pl.Buffered(1))]
                + _split_rows(tm, d, lambda i: i)
                + [pl.BlockSpec((None, 1, d), lambda i: (layer, 0, 0)), gate])
    args = [g, g, w_out_bf, x, x, g_post.reshape(depth, 1, d), mods]
    out_specs = [pl.BlockSpec((tm, d), row)]
    out_shape = [jax.ShapeDtypeStruct((t, d), F32)]
    if has_next:
        sh, sc, _ = _mod_specs(layer + 1, d, row_fn)
        in_specs += [pl.BlockSpec((None, 1, d), lambda i: (layer + 1, 0, 0)), sh, sc]
        args += [g_pre.reshape(depth, 1, d), mods, mods]
        out_specs.append(pl.BlockSpec((tm, d), row))
        out_shape.append(jax.ShapeDtypeStruct((t, d), BF16))
    outs = pl.pallas_call(
        functools.partial(_outproj_kernel, has_next),
        grid=(t // tm,),
        in_specs=in_specs,
        out_specs=out_specs,
        out_shape=out_shape,
        compiler_params=_params(1),
        name="outproj",
    )(*args)
    return (outs[0], outs[1]) if has_next else (outs[0], None)


def _conv_kernel(taps, conformer, ts, n_col, u_ref, up_ref, un_ref, aux_ref, w_ref, *refs):
    if conformer:
        cb_ref, lg_ref, lb_ref, o_ref, wpk, wb, cbuf = refs
    else:
        o_ref, wpk, wb, cbuf = refs
    s = pl.program_id(1)
    has_prev = s > 0
    has_next = s < pl.num_programs(1) - 1
    pad = taps // 2
    half = ts // 2

    @pl.when((pl.program_id(0) == 0) & (s == 0))
    def _broadcast_weights():
        def body(c, carry):
            for k in range(taps):
                wb[c, k] = jnp.broadcast_to(w_ref[c, k:k + 1, :], (BF16_ROWS, LANES)).astype(BF16)
            return carry
        lax.fori_loop(0, n_col, body, 0)

    def pack(lo, hi):
        return pltpu.pack_elementwise([lo.astype(F32), hi.astype(F32)], packed_dtype=BF16)

    for c in range(n_col):
        sl = slice(c * LANES, (c + 1) * LANES)
        prev = jnp.where(has_prev, up_ref[:, sl].astype(F32), 0.0)
        nxt = jnp.where(has_next, un_ref[:, sl].astype(F32), 0.0)
        wpk[c, 0:HALO, :] = pack(prev, u_ref[half - HALO:half, sl])
        wpk[c, HALO:HALO + half, :] = pack(u_ref[0:half, sl], u_ref[half:ts, sl])
        wpk[c, HALO + half:2 * HALO + half, :] = pack(u_ref[half:half + HALO, sl], nxt)

    def col_body(c, carry):
        def chunk_body(j, carry2):
            w0 = pl.multiple_of(j * CONV_WORDS, CONV_WORDS)
            acc = None
            for k in range(taps):
                x = pltpu.bitcast(wpk[c, pl.ds(w0 + (HALO - pad + k), CONV_WORDS), :], BF16)
                wv = jnp.concatenate([wb[c, k]] * (2 * CONV_WORDS // BF16_ROWS), axis=0)
                term = x.astype(F32) * wv.astype(F32)
                acc = term if acc is None else acc + term
            if conformer:
                acc = acc + cb_ref[c]
            cbuf[c, pl.ds(pl.multiple_of(2 * w0, 2 * CONV_WORDS), 2 * CONV_WORDS), :] = acc
            return carry2
        lax.fori_loop(0, half // CONV_WORDS, chunk_body, 0)
        return carry

    lax.fori_loop(0, n_col, col_body, 0)

    if conformer:
        e = n_col * LANES
        tot = cbuf[0]
        for c in range(1, n_col):
            tot = tot + cbuf[c]
        mu = jnp.sum(tot, axis=-1, keepdims=True) / e
        d0 = cbuf[0] - mu
        tot = d0 * d0
        for c in range(1, n_col):
            dc = cbuf[c] - mu
            tot = tot + dc * dc
        rstd = lax.rsqrt(jnp.sum(tot, axis=-1, keepdims=True) / e + EPS)
        for c in range(n_col):
            cbuf[c] = _silu((cbuf[c] - mu) * rstd * lg_ref[c] + lb_ref[c])
    for c in range(n_col):
        sl = slice(c * LANES, (c + 1) * LANES)
        for i in range(2):
            y = cbuf[c, pl.ds(i, half, stride=2), :]
            rows = slice(i * half, (i + 1) * half)
            o_ref[rows, sl] = (y * aux_ref[rows, sl].astype(F32)).astype(BF16)


def _cols(v, n_col):
    lead = v.shape[:-1]
    v = v.reshape(lead + (n_col, LANES))
    return jnp.moveaxis(v, -2, 0)


def _conv_mixer(u, aux, nb, seq, conv_w, conformer, conv_b=None, ln_g=None, ln_b=None):
    t, e = u.shape
    taps = conv_w.shape[0]
    n_col = e // LANES
    ts = 256
    ns = seq // ts
    hb = ts // HALO
    n_halo = seq // HALO
    u3 = u.reshape(nb, seq, e)
    aux3 = aux.reshape(nb, seq, e)
    tile = pl.BlockSpec((None, ts, e), lambda b, s: (b, s, 0))
    prev = pl.BlockSpec((None, HALO, e), lambda b, s: (b, jnp.maximum(s * hb - 1, 0), 0))
    nxt = pl.BlockSpec((None, HALO, e), lambda b, s: (b, jnp.minimum((s + 1) * hb, n_halo - 1), 0))
    in_specs = [tile, prev, nxt, tile,
                pl.BlockSpec((n_col, taps, LANES), lambda b, s: (0, 0, 0))]
    args = [u3, u3, u3, aux3, _cols(conv_w, n_col)]
    if conformer:
        vec = pl.BlockSpec((n_col, 1, LANES), lambda b, s: (0, 0, 0))
        in_specs += [vec, vec, vec]
        args += [_cols(conv_b[None, :], n_col), _cols(ln_g[None, :], n_col), _cols(ln_b[None, :], n_col)]
    scratch = [pltpu.VMEM((n_col, ts // 2 + 2 * HALO, LANES), jnp.uint32),
               pltpu.VMEM((n_col, taps, BF16_ROWS, LANES), BF16),
               pltpu.VMEM((n_col, ts, LANES), F32)]
    out = pl.pallas_call(
        functools.partial(_conv_kernel, taps, conformer, ts, n_col),
        grid=(nb, ns),
        in_specs=in_specs,
        out_specs=tile,
        out_shape=jax.ShapeDtypeStruct((nb, seq, e), BF16),
        scratch_shapes=scratch,
        compiler_params=_params(2),
        name="conformer_conv" if conformer else "short_conv",
    )(*args)
    return out.reshape(t, e)


def _nt_dot(a, b):
    return lax.dot_general(a, b, (((1,), (1,)), ((), ())), preferred_element_type=F32)


def _ctx_attn_kernel(n_heads, q_ref, k_ref, v_ref, sz_ref, o_ref):
    for hh in range(n_heads):
        sl = slice(hh * HEAD_DIM, (hh + 1) * HEAD_DIM)
        s = _nt_dot(q_ref[:, sl], k_ref[:, sl])
        p = jnp.exp(s - jnp.max(s, axis=-1, keepdims=True))
        l = jnp.sum(p, axis=-1, keepdims=True)
        o = jnp.dot(p.astype(BF16), v_ref[:, sl], preferred_element_type=F32) / l
        o_ref[:, sl] = (o * sz_ref[:, sl].astype(F32)).astype(BF16)


def _ctx_attention(q, k, v, sz, nb, seq):
    t, e = q.shape
    heads_per_step = 16
    w = heads_per_step * HEAD_DIM
    blk = pl.BlockSpec((seq, w), lambda b, g: (b, g))
    return pl.pallas_call(
        functools.partial(_ctx_attn_kernel, heads_per_step),
        grid=(nb, e // w),
        in_specs=[blk] * 4,
        out_specs=blk,
        out_shape=jax.ShapeDtypeStruct((t, e), BF16),
        compiler_params=_params(2),
        name="ctx_attention",
    )(q, k, v, sz)


def _nbr_geometry(rows):
    n_blocks = rows // Q_ROWS
    classes = {}
    for name, m in (("first", 0), ("mid", 1), ("last", n_blocks - 1)):
        kr0 = int(np.clip(Q_ROWS * m - WIN_H // 2, 0, rows - K_ROWS))
        per_row = []
        for i in range(Q_ROWS):
            r = Q_ROWS * m + i
            r0 = int(np.clip(r - WIN_H // 2, 0, rows - WIN_H))
            a = r0 - kr0
            assert 0 <= a and a + WIN_H <= K_ROWS
            jp0, jp1 = a // 2, (a + WIN_H - 1) // 2
            pairs = []
            for jp in range(jp0, jp1 + 1):
                halves = []
                for j in (2 * jp, 2 * jp + 1):
                    ok = a <= j < a + WIN_H
                    halves.append(kr0 + j - r + WIN_H - 1 if ok else None)
                pairs.append(tuple(halves))
            per_row.append((jp0, pairs))
        classes[name] = (kr0 - Q_ROWS * m, per_row)
    return classes


def _nbr_kernel(geom, tab_index, n_bias_rows, n_bias_cols,
                rpb_ref, q_ref, k_ref, v_ref, sz_ref, kc_ref, vc_ref, o_ref,
                half_l, half_r, tabs, s_loc, s_ctx, p_loc, p_ctx, linv, kcb, vcb):
    h = pl.program_id(0)
    b = pl.program_id(1)
    n_blocks = q_ref.shape[0] // (Q_ROWS * GRID_W)
    nq = Q_ROWS * GRID_W
    nk = K_ROWS * GRID_W

    @pl.when(b == 0)
    def _build_bias_tables():
        qc = lax.broadcasted_iota(jnp.int32, (GRID_W, 2 * GRID_W), 0)
        lane = lax.broadcasted_iota(jnp.int32, (GRID_W, 2 * GRID_W), 1)
        kc = lane & (GRID_W - 1)
        right = lane >= GRID_W
        cstart = jnp.clip(qc - WIN_W // 2, 0, GRID_W - WIN_W)
        col_ok = (kc >= cstart) & (kc < cstart + WIN_W)
        dcol = kc - qc + (WIN_W - 1)
        code_l = jnp.where(col_ok & jnp.logical_not(right), dcol, -1)
        code_r = jnp.where(col_ok & right, dcol, -1)
        base = h * (n_bias_rows * n_bias_cols)
        for d in range(n_bias_rows):
            acc_l = jnp.full((GRID_W, 2 * GRID_W), NEG_INF, F32)
            acc_r = acc_l
            for dc in range(n_bias_cols):
                val = rpb_ref[base + d * n_bias_cols + dc]
                acc_l = jnp.where(code_l == dc, val, acc_l)
                acc_r = jnp.where(code_r == dc, val, acc_r)
            half_l[d] = acc_l
            half_r[d] = acc_r
        for (dl, dr), t in tab_index.items():
            if dl is not None and dr is not None:
                tabs[t] = jnp.maximum(half_l[dl], half_r[dr])
            elif dl is not None:
                tabs[t] = half_l[dl]
            else:
                tabs[t] = half_r[dr]

    kcb[...] = kc_ref[...].astype(BF16)
    vcb[...] = vc_ref[...].astype(BF16)

    n_ctx = kcb.shape[0] // LANES

    def geometry(m):
        cls = "first" if m == 0 else ("last" if m == n_blocks - 1 else "mid")
        k_off, per_row = geom[cls]
        return m * nq, m * nq + k_off * GRID_W, per_row

    def scores(m):
        q0, k0, _ = geometry(m)
        q = q_ref[q0:q0 + nq, :]
        s_loc[m % 2] = _nt_dot(q, k_ref[k0:k0 + nk, :])
        s_ctx[m % 2] = _nt_dot(q, kcb[...])

    def softmax(m):
        _, _, per_row = geometry(m)
        sl_ref, sc_ref, pl_ref, pc_ref = s_loc.at[m % 2], s_ctx.at[m % 2], p_loc.at[m % 2], p_ctx.at[m % 2]
        for i in range(Q_ROWS):
            rs = slice(i * GRID_W, (i + 1) * GRID_W)
            jp0, pairs = per_row[i]
            lo, hi = jp0 * LANES, (jp0 + len(pairs)) * LANES

            def local(t):
                cs = slice(lo + t * LANES, lo + (t + 1) * LANES)
                return cs, sl_ref[rs, cs] + tabs[tab_index[pairs[t]]]

            def context(t):
                cs = slice(t * LANES, (t + 1) * LANES)
                return cs, sc_ref[rs, cs]

            vmax = context(0)[1]
            for t in range(1, n_ctx):
                vmax = jnp.maximum(vmax, context(t)[1])
            for t in range(len(pairs)):
                vmax = jnp.maximum(vmax, local(t)[1])
            mx = jnp.max(vmax, axis=-1, keepdims=True)
            vsum = jnp.zeros((GRID_W, LANES), F32)
            for t in range(n_ctx):
                cs, x = context(t)
                px = jnp.exp(x - mx)
                vsum = vsum + px
                pc_ref[rs, cs] = px.astype(BF16)
            for t in range(len(pairs)):
                cs, x = local(t)
                px = jnp.exp(x - mx)
                vsum = vsum + px
                pl_ref[rs, cs] = px.astype(BF16)
            if lo > 0:
                pl_ref[rs, 0:lo] = jnp.zeros((GRID_W, lo), BF16)
            if hi < nk:
                pl_ref[rs, hi:nk] = jnp.zeros((GRID_W, nk - hi), BF16)
            l = jnp.sum(vsum, axis=-1, keepdims=True)
            linv[m % 2, rs, :] = jnp.broadcast_to(1.0 / l, (GRID_W, HEAD_DIM))

    def values(m):
        q0, k0, _ = geometry(m)
        o = jnp.dot(p_loc[m % 2], v_ref[k0:k0 + nk, :], preferred_element_type=F32)
        o = o + jnp.dot(p_ctx[m % 2], vcb[...], preferred_element_type=F32)
        o = o * linv[m % 2] * sz_ref[q0:q0 + nq, :].astype(F32)
        o_ref[q0:q0 + nq, :] = o.astype(BF16)

    scores(0)
    for m in range(n_blocks):
        if m + 1 < n_blocks:
            scores(m + 1)
        softmax(m)
        values(m)


def _nbr_attention(q, k, v, sz, cache_k, cache_v, j, rpb):
    t, e = q.shape
    nb, _, past, n_heads, hd = cache_k.shape
    seq = t // nb
    rows = seq // GRID_W
    assert hd == HEAD_DIM and e == n_heads * HEAD_DIM and seq == rows * GRID_W
    assert rows % Q_ROWS == 0 and rows >= 3 * Q_ROWS and GRID_W * 2 == LANES
    n_bias_rows, n_bias_cols = rpb.shape[1], rpb.shape[2]
    assert n_bias_rows == 2 * WIN_H - 1 and n_bias_cols == 2 * WIN_W - 1
    geom = _nbr_geometry(rows)
    keys = sorted({pr for _, per_row in geom.values() for _, pairs in per_row for pr in pairs},
                  key=lambda pr: (pr[0] is None, pr[1] is None, pr))
    tab_index = {pr: n for n, pr in enumerate(keys)}
    nq, nk = Q_ROWS * GRID_W, K_ROWS * GRID_W
    head = pl.BlockSpec((seq, HEAD_DIM), lambda h, b: (b, h))
    ctx = pl.BlockSpec((None, None, past, HEAD_DIM), lambda h, b: (b, j, 0, h))
    ck = cache_k.reshape(nb, cache_k.shape[1], past, e)
    cv = cache_v.reshape(nb, cache_v.shape[1], past, e)
    return pl.pallas_call(
        functools.partial(_nbr_kernel, geom, tab_index, n_bias_rows, n_bias_cols),
        grid=(n_heads, nb),
        in_specs=[pl.BlockSpec(memory_space=pltpu.SMEM), head, head, head, head, ctx, ctx],
        out_specs=head,
        out_shape=jax.ShapeDtypeStruct((t, e), BF16),
        scratch_shapes=[pltpu.VMEM((n_bias_rows, GRID_W, LANES), F32),
                        pltpu.VMEM((n_bias_rows, GRID_W, LANES), F32),
                        pltpu.VMEM((len(keys), GRID_W, LANES), F32),
                        pltpu.VMEM((2, nq, nk), F32),
                        pltpu.VMEM((2, nq, past), F32),
                        pltpu.VMEM((2, nq, nk), BF16),
                        pltpu.VMEM((2, nq, past), BF16),
                        pltpu.VMEM((2, nq, HEAD_DIM), F32),
                        pltpu.VMEM((past, HEAD_DIM), BF16),
                        pltpu.VMEM((past, HEAD_DIM), BF16)],
        compiler_params=_params(2),
        name="nbr_attention",
    )(rpb.reshape(-1), q, k, v, sz, ck, cv)


def kernel(x_prompt, x_sample, cache_k, cache_v, c, c_ctx, w_ada, b_ada, g_pre, g_post,
           w_in_a, conv_w_a, conv_b_a, ln_g_a, ln_b_a, w_in_b, rpb_b, w_in_c, conv_w_c, w_out):
    depth = w_ada.shape[0]
    d = x_prompt.shape[-1]
    tm = 512
    tm_pre = 1024
    mods = _modulation(c, c_ctx, w_ada, b_ada)
    w_out_bf = w_out.astype(BF16)

    groups = []
    for x, is_ctx in ((x_prompt, True), (x_sample, False)):
        nb, seq, _ = x.shape
        if is_ctx:
            row_fn = lambda tile: (lambda i: 0)
        else:
            row_fn = functools.partial(lambda tile, per: (lambda i: 1 + i // (per // tile)), per=seq)
        x2 = x.reshape(nb * seq, d)
        groups.append(dict(x=x2, h=_prenorm(x2, mods, g_pre, 0, row_fn(tm_pre), tm_pre),
                           nb=nb, seq=seq, row_fn=row_fn(tm), is_ctx=is_ctx))

    new_k, new_v = [], []
    for i in range(depth):
        kind, j = i % N_MIXERS, i // N_MIXERS
        for grp in groups:
            nb, seq = grp["nb"], grp["seq"]
            if kind == 0:
                u, sz = _inproj(grp["h"], w_in_a, j, 0)
                g = _conv_mixer(u, sz, nb, seq, conv_w_a[j], True, conv_b_a[j], ln_g_a[j], ln_b_a[j])
            elif kind == 1:
                if grp["is_ctx"]:
                    q, k, v, sz, k32, v32 = _inproj(grp["h"], w_in_b, j, 1, emit_kv=True)
                    g = _ctx_attention(q, k, v, sz, nb, seq)
                    new_k.append(k32.reshape(nb, seq, -1, HEAD_DIM))
                    new_v.append(v32.reshape(nb, seq, -1, HEAD_DIM))
                else:
                    q, k, v, sz = _inproj(grp["h"], w_in_b, j, 1)
                    g = _nbr_attention(q, k, v, sz, cache_k, cache_v, j, rpb_b[j])
            else:
                cv, bz = _inproj(grp["h"], w_in_c, j, 2)
                g = _conv_mixer(cv, bz, nb, seq, conv_w_c[j], False)
            grp["x"], grp["h"] = _outproj(g, w_out_bf, i, grp["x"], mods, g_post, g_pre, grp["row_fn"], tm)

    y_prompt = groups[0]["x"].reshape(x_prompt.shape)
    y_sample = groups[1]["x"].reshape(x_sample.shape)
    return (y_prompt, y_sample, jnp.stack(new_k, axis=1), jnp.stack(new_v, axis=1))
```

```python
import functools

import numpy as np
import jax
import jax.numpy as jnp
from jax import lax
from jax.experimental import pallas as pl
from jax.experimental.pallas import tpu as pltpu

F32 = jnp.float32
BF16 = jnp.bfloat16

GRID_W = 64
HEAD_DIM = 128
WIN_H = 8
WIN_W = 16
N_MIXERS = 3
EPS = 1e-6
NEG_INF = -1e30
ATTN_SCALE = HEAD_DIM ** -0.5
LOG2_E = 1.4426950408889634

LANES = 128
BF16_ROWS = 16
HALO = BF16_ROWS
CONV_WORDS = 64
MOD_ROWS = 8
VMEM_LIMIT = 56 * 1024 * 1024
SLAB_ROWS = 256
Q_ROWS = 8
K_ROWS = 16


def _params(n_axes):
    return pltpu.CompilerParams(dimension_semantics=("arbitrary",) * n_axes,
                                vmem_limit_bytes=VMEM_LIMIT)


def _sigmoid(x):
    return 1.0 / (1.0 + jnp.exp2(x * (-LOG2_E)))


def _silu(x):
    return x * _sigmoid(x)


def _mod_kernel(cond_ref, w_ref, b_ref, o_ref):
    s = _silu(cond_ref[...])
    o_ref[...] = jnp.dot(s, w_ref[...], preferred_element_type=F32,
                         precision=lax.Precision.HIGHEST) + b_ref[...]


def _modulation(c, c_ctx, w_ada, b_ada):
    depth, d, d3 = w_ada.shape
    pad = MOD_ROWS - 1 - c.shape[0]
    cond = jnp.concatenate([c_ctx[None, :], c, jnp.zeros((pad, d), F32)], axis=0)
    tn = 1024
    mods = pl.pallas_call(
        _mod_kernel,
        grid=(depth, d3 // tn),
        in_specs=[pl.BlockSpec((MOD_ROWS, d), lambda l, n: (0, 0)),
                  pl.BlockSpec((None, d, tn), lambda l, n: (l, 0, n)),
                  pl.BlockSpec((None, 1, tn), lambda l, n: (l, 0, n))],
        out_specs=pl.BlockSpec((None, MOD_ROWS, tn), lambda l, n: (l, 0, n)),
        out_shape=jax.ShapeDtypeStruct((depth, MOD_ROWS, d3), F32),
        compiler_params=_params(2),
        name="modulation",
    )(cond, w_ada, b_ada.reshape(depth, 1, d3))
    return mods.reshape(depth * MOD_ROWS, 1, d3)


def _mod_specs(layer, d, row_fn):
    def spec(col):
        return pl.BlockSpec((None, 1, d), lambda i: (layer * MOD_ROWS + row_fn(i), 0, col))
    return spec(0), spec(1), spec(2)


def _adaln(x, g, shift, scale):
    y = x * lax.rsqrt(jnp.mean(x * x, axis=-1, keepdims=True) + EPS)
    return (y * g) * (1.0 + scale) + shift


def _split_rows(tm, cols, index_fn, n_parts=2):
    part = tm // n_parts
    return [pl.BlockSpec((part, cols), functools.partial(
        lambda *idx, p: (index_fn(*idx) * n_parts + p, 0), p=p)) for p in range(n_parts)]


def _prenorm_kernel(n_parts, *refs):
    x_refs = refs[:n_parts]
    g_ref, sh_ref, sc_ref, h_ref = refs[n_parts:]
    rows = x_refs[0].shape[0]
    for p, x_ref in enumerate(x_refs):
        h_ref[p * rows:(p + 1) * rows, :] = _adaln(
            x_ref[...], g_ref[...], sh_ref[...], sc_ref[...]).astype(BF16)


def _prenorm(x, mods, g_pre, layer, row_fn, tm):
    t, d = x.shape
    sh, sc, _ = _mod_specs(layer, d, row_fn)
    return pl.pallas_call(
        functools.partial(_prenorm_kernel, tm // SLAB_ROWS),
        grid=(t // tm,),
        in_specs=_split_rows(tm, d, lambda i: i, tm // SLAB_ROWS) + [
            pl.BlockSpec((None, 1, d), lambda i: (layer, 0, 0)), sh, sc],
        out_specs=pl.BlockSpec((tm, d), lambda i: (i, 0)),
        out_shape=jax.ShapeDtypeStruct((t, d), BF16),
        compiler_params=_params(1),
        name="prenorm",
    )(*([x] * (tm // SLAB_ROWS)), g_pre.reshape(g_pre.shape[0], 1, d), mods, mods)


def _inproj_kernel(kind, n_groups, tn, emit_kv, n_parts, *refs):
    h_refs = refs[:n_parts]
    refs = refs[n_parts:]
    w_refs = refs[:n_groups]
    out_refs = refs[n_groups:-1]
    wbf_ref = refs[-1]

    @pl.when(pl.program_id(1) == 0)
    def _cast_weights():
        for g in range(n_groups):
            wbf_ref[:, g * tn:(g + 1) * tn] = w_refs[g][...].astype(BF16)

    slab = h_refs[0].shape[0]
    for part, h_ref in enumerate(h_refs):
        rows = slice(part * slab, (part + 1) * slab)
        r = jnp.dot(h_ref[...], wbf_ref[...], preferred_element_type=F32)
        p = [r[:, g * tn:(g + 1) * tn] for g in range(n_groups)]
        if kind == 0:
            a, b, z = p
            out_refs[0][rows, :] = (a * _sigmoid(b)).astype(BF16)
            out_refs[1][rows, :] = _silu(z).astype(BF16)
        elif kind == 1:
            q, k, v, z = p
            out_refs[0][rows, :] = (q * ATTN_SCALE).astype(BF16)
            out_refs[1][rows, :] = k.astype(BF16)
            out_refs[2][rows, :] = v.astype(BF16)
            out_refs[3][rows, :] = _silu(z).astype(BF16)
            if emit_kv:
                out_refs[4][rows, :] = k
                out_refs[5][rows, :] = v
        else:
            bg, cg, v, z = p
            out_refs[0][rows, :] = (cg * v).astype(BF16)
            out_refs[1][rows, :] = (bg * _silu(z)).astype(BF16)


def _inproj(h, w_in, j, kind, emit_kv=False):
    t, d = h.shape
    n_groups = 3 if kind == 0 else 4
    e = w_in.shape[-1] // n_groups
    wide = kind == 0
    tm = min(1024 if (emit_kv or wide) else 2048, t)
    tn = 512 if wide else 256
    nj = e // tn
    w_specs = [pl.BlockSpec((None, d, tn), functools.partial(lambda jj, ii, g: (j, 0, g * nj + jj), g=g))
               for g in range(n_groups)]
    n_bf = {0: 2, 1: 4, 2: 2}[kind]
    out_shape = [jax.ShapeDtypeStruct((t, e), BF16)] * n_bf
    if emit_kv:
        out_shape = out_shape + [jax.ShapeDtypeStruct((t, e), F32)] * 2
    out_spec = pl.BlockSpec((tm, tn), lambda jj, ii: (ii, jj))
    return pl.pallas_call(
        functools.partial(_inproj_kernel, kind, n_groups, tn, emit_kv, tm // SLAB_ROWS),
        grid=(nj, t // tm),
        in_specs=_split_rows(tm, d, lambda jj, ii: ii, tm // SLAB_ROWS) + w_specs,
        out_specs=[out_spec] * len(out_shape),
        out_shape=out_shape,
        scratch_shapes=[pltpu.VMEM((d, n_groups * tn), BF16)],
        compiler_params=_params(2),
        name="inproj",
    )(*([h] * (tm // SLAB_ROWS)), *([w_in] * n_groups))


def _outproj_kernel(has_next, g0_ref, g1_ref, w_ref, x0_ref, x1_ref, gpost_ref, gate_ref, *refs):
    if has_next:
        gpre_ref, sh_ref, sc_ref, xo_ref, h_ref = refs
    else:
        (xo_ref,) = refs
    half = g0_ref.shape[0]
    for part, (g_ref, x_ref) in enumerate(((g0_ref, x0_ref), (g1_ref, x1_ref))):
        rows = slice(part * half, (part + 1) * half)
        m = jnp.dot(g_ref[...], w_ref[...], preferred_element_type=F32)
        y = m * lax.rsqrt(jnp.mean(m * m, axis=-1, keepdims=True) + EPS)
        xn = x_ref[...] + gate_ref[...] * (y * gpost_ref[...])
        if has_next:
            h_ref[rows, :] = _adaln(xn, gpre_ref[...], sh_ref[...], sc_ref[...]).astype(BF16)
        xo_ref[rows, :] = xn


def _outproj(g, w_out_bf, layer, x, mods, g_post, g_pre, row_fn, tm):
    t, e = g.shape
    d = x.shape[1]
    depth = w_out_bf.shape[0]
    has_next = layer + 1 < depth
    _, _, gate = _mod_specs(layer, d, row_fn)
    row = lambda i: (i, 0)
    in_specs = (_split_rows(tm, e, lambda i: i)
                + [pl.BlockSpec((None, e, d), lambda i: (layer, 0, 0), pipeline_mode=pl.Buffered(1))]
                + _split_rows(tm, d, lambda i: i)
                + [pl.BlockSpec((None, 1, d), lambda i: (layer, 0, 0)), gate])
    args = [g, g, w_out_bf, x, x, g_post.reshape(depth, 1, d), mods]
    out_specs = [pl.BlockSpec((tm, d), row)]
    out_shape = [jax.ShapeDtypeStruct((t, d), F32)]
    if has_next:
        sh, sc, _ = _mod_specs(layer + 1, d, row_fn)
        in_specs += [pl.BlockSpec((None, 1, d), lambda i: (layer + 1, 0, 0)), sh, sc]
        args += [g_pre.reshape(depth, 1, d), mods, mods]
        out_specs.append(pl.BlockSpec((tm, d), row))
        out_shape.append(jax.ShapeDtypeStruct((t, d), BF16))
    outs = pl.pallas_call(
        functools.partial(_outproj_kernel, has_next),
        grid=(t // tm,),
        in_specs=in_specs,
        out_specs=out_specs,
        out_shape=out_shape,
        compiler_params=_params(1),
        name="outproj",
    )(*args)
    return (outs[0], outs[1]) if has_next else (outs[0], None)


def _conv_kernel(taps, conformer, ts, n_col, u_ref, up_ref, un_ref, aux_ref, w_ref, *refs):
    if conformer:
        cb_ref, lg_ref, lb_ref, o_ref, wpk, wb, cbuf = refs
    else:
        o_ref, wpk, wb, cbuf = refs
    s = pl.program_id(1)
    has_prev = s > 0
    has_next = s < pl.num_programs(1) - 1
    pad = taps // 2
    half = ts // 2

    @pl.when((pl.program_id(0) == 0) & (s == 0))
    def _broadcast_weights():
        def body(c, carry):
            for k in range(taps):
                wb[c, k] = jnp.broadcast_to(w_ref[c, k:k + 1, :], (BF16_ROWS, LANES)).astype(BF16)
            return carry
        lax.fori_loop(0, n_col, body, 0)

    def pack(lo, hi):
        return pltpu.pack_elementwise([lo.astype(F32), hi.astype(F32)], packed_dtype=BF16)

    for c in range(n_col):
        sl = slice(c * LANES, (c + 1) * LANES)
        prev = jnp.where(has_prev, up_ref[:, sl].astype(F32), 0.0)
        nxt = jnp.where(has_next, un_ref[:, sl].astype(F32), 0.0)
        wpk[c, 0:HALO, :] = pack(prev, u_ref[half - HALO:half, sl])
        wpk[c, HALO:HALO + half, :] = pack(u_ref[0:half, sl], u_ref[half:ts, sl])
        wpk[c, HALO + half:2 * HALO + half, :] = pack(u_ref[half:half + HALO, sl], nxt)

    def col_body(c, carry):
        def chunk_body(j, carry2):
            w0 = pl.multiple_of(j * CONV_WORDS, CONV_WORDS)
            acc = None
            for k in range(taps):
                x = pltpu.bitcast(wpk[c, pl.ds(w0 + (HALO - pad + k), CONV_WORDS), :], BF16)
                wv = jnp.concatenate([wb[c, k]] * (2 * CONV_WORDS // BF16_ROWS), axis=0)
                term = x.astype(F32) * wv.astype(F32)
                acc = term if acc is None else acc + term
            if conformer:
                acc = acc + cb_ref[c]
            cbuf[c, pl.ds(pl.multiple_of(2 * w0, 2 * CONV_WORDS), 2 * CONV_WORDS), :] = acc
            return carry2
        lax.fori_loop(0, half // CONV_WORDS, chunk_body, 0)
        return carry

    lax.fori_loop(0, n_col, col_body, 0)

    if conformer:
        e = n_col * LANES
        tot = cbuf[0]
        for c in range(1, n_col):
            tot = tot + cbuf[c]
        mu = jnp.sum(tot, axis=-1, keepdims=True) / e
        d0 = cbuf[0] - mu
        tot = d0 * d0
        for c in range(1, n_col):
            dc = cbuf[c] - mu
            tot = tot + dc * dc
        rstd = lax.rsqrt(jnp.sum(tot, axis=-1, keepdims=True) / e + EPS)
        for c in range(n_col):
            cbuf[c] = _silu((cbuf[c] - mu) * rstd * lg_ref[c] + lb_ref[c])
    for c in range(n_col):
        sl = slice(c * LANES, (c + 1) * LANES)
        for i in range(2):
            y = cbuf[c, pl.ds(i, half, stride=2), :]
            rows = slice(i * half, (i + 1) * half)
            o_ref[rows, sl] = (y * aux_ref[rows, sl].astype(F32)).astype(BF16)


def _cols(v, n_col):
    lead = v.shape[:-1]
    v = v.reshape(lead + (n_col, LANES))
    return jnp.moveaxis(v, -2, 0)


def _conv_mixer(u, aux, nb, seq, conv_w, conformer, conv_b=None, ln_g=None, ln_b=None):
    t, e = u.shape
    taps = conv_w.shape[0]
    n_col = e // LANES
    ts = 256
    ns = seq // ts
    hb = ts // HALO
    n_halo = seq // HALO
    u3 = u.reshape(nb, seq, e)
    aux3 = aux.reshape(nb, seq, e)
    tile = pl.BlockSpec((None, ts, e), lambda b, s: (b, s, 0))
    prev = pl.BlockSpec((None, HALO, e), lambda b, s: (b, jnp.maximum(s * hb - 1, 0), 0))
    nxt = pl.BlockSpec((None, HALO, e), lambda b, s: (b, jnp.minimum((s + 1) * hb, n_halo - 1), 0))
    in_specs = [tile, prev, nxt, tile,
                pl.BlockSpec((n_col, taps, LANES), lambda b, s: (0, 0, 0))]
    args = [u3, u3, u3, aux3, _cols(conv_w, n_col)]
    if conformer:
        vec = pl.BlockSpec((n_col, 1, LANES), lambda b, s: (0, 0, 0))
        in_specs += [vec, vec, vec]
        args += [_cols(conv_b[None, :], n_col), _cols(ln_g[None, :], n_col), _cols(ln_b[None, :], n_col)]
    scratch = [pltpu.VMEM((n_col, ts // 2 + 2 * HALO, LANES), jnp.uint32),
               pltpu.VMEM((n_col, taps, BF16_ROWS, LANES), BF16),
               pltpu.VMEM((n_col, ts, LANES), F32)]
    out = pl.pallas_call(
        functools.partial(_conv_kernel, taps, conformer, ts, n_col),
        grid=(nb, ns),
        in_specs=in_specs,
        out_specs=tile,
        out_shape=jax.ShapeDtypeStruct((nb, seq, e), BF16),
        scratch_shapes=scratch,
        compiler_params=_params(2),
        name="conformer_conv" if conformer else "short_conv",
    )(*args)
    return out.reshape(t, e)


def _nt_dot(a, b):
    return lax.dot_general(a, b, (((1,), (1,)), ((), ())), preferred_element_type=F32)


def _ctx_attn_kernel(n_heads, q_ref, k_ref, v_ref, sz_ref, o_ref):
    for hh in range(n_heads):
        sl = slice(hh * HEAD_DIM, (hh + 1) * HEAD_DIM)
        s = _nt_dot(q_ref[:, sl], k_ref[:, sl])
        p = jnp.exp(s - jnp.max(s, axis=-1, keepdims=True))
        l = jnp.sum(p, axis=-1, keepdims=True)
        o = jnp.dot(p.astype(BF16), v_ref[:, sl], preferred_element_type=F32) / l
        o_ref[:, sl] = (o * sz_ref[:, sl].astype(F32)).astype(BF16)


def _ctx_attention(q, k, v, sz, nb, seq):
    t, e = q.shape
    heads_per_step = 16
    w = heads_per_step * HEAD_DIM
    blk = pl.BlockSpec((seq, w), lambda b, g: (b, g))
    return pl.pallas_call(
        functools.partial(_ctx_attn_kernel, heads_per_step),
        grid=(nb, e // w),
        in_specs=[blk] * 4,
        out_specs=blk,
        out_shape=jax.ShapeDtypeStruct((t, e), BF16),
        compiler_params=_params(2),
        name="ctx_attention",
    )(q, k, v, sz)


def _nbr_geometry(rows):
    n_blocks = rows // Q_ROWS
    classes = {}
    for name, m in (("first", 0), ("mid", 1), ("last", n_blocks - 1)):
        kr0 = int(np.clip(Q_ROWS * m - WIN_H // 2, 0, rows - K_ROWS))
        per_row = []
        for i in range(Q_ROWS):
            r = Q_ROWS * m + i
            r0 = int(np.clip(r - WIN_H // 2, 0, rows - WIN_H))
            a = r0 - kr0
            assert 0 <= a and a + WIN_H <= K_ROWS
            jp0, jp1 = a // 2, (a + WIN_H - 1) // 2
            pairs = []
            for jp in range(jp0, jp1 + 1):
                halves = []
                for j in (2 * jp, 2 * jp + 1):
                    ok = a <= j < a + WIN_H
                    halves.append(kr0 + j - r + WIN_H - 1 if ok else None)
                pairs.append(tuple(halves))
            per_row.append((jp0, pairs))
        classes[name] = (kr0 - Q_ROWS * m, per_row)
    return classes


def _nbr_kernel(geom, tab_index, n_bias_rows, n_bias_cols,
                rpb_ref, q_ref, k_ref, v_ref, sz_ref, kc_ref, vc_ref, o_ref,
                half_l, half_r, tabs, s_loc, s_ctx, p_loc, p_ctx, linv, kcb, vcb):
    h = pl.program_id(0)
    b = pl.program_id(1)
    n_blocks = q_ref.shape[0] // (Q_ROWS * GRID_W)
    nq = Q_ROWS * GRID_W
    nk = K_ROWS * GRID_W

    @pl.when(b == 0)
    def _build_bias_tables():
        qc = lax.broadcasted_iota(jnp.int32, (GRID_W, 2 * GRID_W), 0)
        lane = lax.broadcasted_iota(jnp.int32, (GRID_W, 2 * GRID_W), 1)
        kc = lane & (GRID_W - 1)
        right = lane >= GRID_W
        cstart = jnp.clip(qc - WIN_W // 2, 0, GRID_W - WIN_W)
        col_ok = (kc >= cstart) & (kc < cstart + WIN_W)
        dcol = kc - qc + (WIN_W - 1)
        code_l = jnp.where(col_ok & jnp.logical_not(right), dcol, -1)
        code_r = jnp.where(col_ok & right, dcol, -1)
        base = h * (n_bias_rows * n_bias_cols)
        for d in range(n_bias_rows):
            acc_l = jnp.full((GRID_W, 2 * GRID_W), NEG_INF, F32)
            acc_r = acc_l
            for dc in range(n_bias_cols):
                val = rpb_ref[base + d * n_bias_cols + dc]
                acc_l = jnp.where(code_l == dc, val, acc_l)
                acc_r = jnp.where(code_r == dc, val, acc_r)
            half_l[d] = acc_l
            half_r[d] = acc_r
        for (dl, dr), t in tab_index.items():
            if dl is not None and dr is not None:
                tabs[t] = jnp.maximum(half_l[dl], half_r[dr])
            elif dl is not None:
                tabs[t] = half_l[dl]
            else:
                tabs[t] = half_r[dr]

    kcb[...] = kc_ref[...].astype(BF16)
    vcb[...] = vc_ref[...].astype(BF16)

    n_ctx = kcb.shape[0] // LANES

    def geometry(m):
        cls = "first" if m == 0 else ("last" if m == n_blocks - 1 else "mid")
        k_off, per_row = geom[cls]
        return m * nq, m * nq + k_off * GRID_W, per_row

    def scores(m):
        q0, k0, _ = geometry(m)
        q = q_ref[q0:q0 + nq, :]
        s_loc[m % 2] = _nt_dot(q, k_ref[k0:k0 + nk, :])
        s_ctx[m % 2] = _nt_dot(q, kcb[...])

    def softmax(m):
        _, _, per_row = geometry(m)
        sl_ref, sc_ref, pl_ref, pc_ref = s_loc.at[m % 2], s_ctx.at[m % 2], p_loc.at[m % 2], p_ctx.at[m % 2]
        for i in range(Q_ROWS):
            rs = slice(i * GRID_W, (i + 1) * GRID_W)
            jp0, pairs = per_row[i]
            lo, hi = jp0 * LANES, (jp0 + len(pairs)) * LANES

            def local(t):
                cs = slice(lo + t * LANES, lo + (t + 1) * LANES)
                return cs, sl_ref[rs, cs] + tabs[tab_index[pairs[t]]]

            def context(t):
                cs = slice(t * LANES, (t + 1) * LANES)
                return cs, sc_ref[rs, cs]

            vmax = context(0)[1]
            for t in range(1, n_ctx):
                vmax = jnp.maximum(vmax, context(t)[1])
            for t in range(len(pairs)):
                vmax = jnp.maximum(vmax, local(t)[1])
            mx = jnp.max(vmax, axis=-1, keepdims=True)
            vsum = jnp.zeros((GRID_W, LANES), F32)
            for t in range(n_ctx):
                cs, x = context(t)
                px = jnp.exp(x - mx)
                vsum = vsum + px
                pc_ref[rs, cs] = px.astype(BF16)
            for t in range(len(pairs)):
                cs, x = local(t)
                px = jnp.exp(x - mx)
                vsum = vsum + px
                pl_ref[rs, cs] = px.astype(BF16)
            if lo > 0:
                pl_ref[rs, 0:lo] = jnp.zeros((GRID_W, lo), BF16)
            if hi < nk:
                pl_ref[rs, hi:nk] = jnp.zeros((GRID_W, nk - hi), BF16)
            l = jnp.sum(vsum, axis=-1, keepdims=True)
            linv[m % 2, rs, :] = jnp.broadcast_to(1.0 / l, (GRID_W, HEAD_DIM))

    def values(m):
        q0, k0, _ = geometry(m)
        o = jnp.dot(p_loc[m % 2], v_ref[k0:k0 + nk, :], preferred_element_type=F32)
        o = o + jnp.dot(p_ctx[m % 2], vcb[...], preferred_element_type=F32)
        o = o * linv[m % 2] * sz_ref[q0:q0 + nq, :].astype(F32)
        o_ref[q0:q0 + nq, :] = o.astype(BF16)

    scores(0)
    for m in range(n_blocks):
        if m + 1 < n_blocks:
            scores(m + 1)
        softmax(m)
        values(m)


def _nbr_attention(q, k, v, sz, cache_k, cache_v, j, rpb):
    t, e = q.shape
    nb, _, past, n_heads, hd = cache_k.shape
    seq = t // nb
    rows = seq // GRID_W
    assert hd == HEAD_DIM and e == n_heads * HEAD_DIM and seq == rows * GRID_W
    assert rows % Q_ROWS == 0 and rows >= 3 * Q_ROWS and GRID_W * 2 == LANES
    n_bias_rows, n_bias_cols = rpb.shape[1], rpb.shape[2]
    assert n_bias_rows == 2 * WIN_H - 1 and n_bias_cols == 2 * WIN_W - 1
    geom = _nbr_geometry(rows)
    keys = sorted({pr for _, per_row in geom.values() for _, pairs in per_row for pr in pairs},
                  key=lambda pr: (pr[0] is None, pr[1] is None, pr))
    tab_index = {pr: n for n, pr in enumerate(keys)}
    nq, nk = Q_ROWS * GRID_W, K_ROWS * GRID_W
    head = pl.BlockSpec((seq, HEAD_DIM), lambda h, b: (b, h))
    ctx = pl.BlockSpec((None, None, past, HEAD_DIM), lambda h, b: (b, j, 0, h))
    ck = cache_k.reshape(nb, cache_k.shape[1], past, e)
    cv = cache_v.reshape(nb, cache_v.shape[1], past, e)
    return pl.pallas_call(
        functools.partial(_nbr_kernel, geom, tab_index, n_bias_rows, n_bias_cols),
        grid=(n_heads, nb),
        in_specs=[pl.BlockSpec(memory_space=pltpu.SMEM), head, head, head, head, ctx, ctx],
        out_specs=head,
        out_shape=jax.ShapeDtypeStruct((t, e), BF16),
        scratch_shapes=[pltpu.VMEM((n_bias_rows, GRID_W, LANES), F32),
                        pltpu.VMEM((n_bias_rows, GRID_W, LANES), F32),
                        pltpu.VMEM((len(keys), GRID_W, LANES), F32),
                        pltpu.VMEM((2, nq, nk), F32),
                        pltpu.VMEM((2, nq, past), F32),
                        pltpu.VMEM((2, nq, nk), BF16),
                        pltpu.VMEM((2, nq, past), BF16),
                        pltpu.VMEM((2, nq, HEAD_DIM), F32),
                        pltpu.VMEM((past, HEAD_DIM), BF16),
                        pltpu.VMEM((past, HEAD_DIM), BF16)],
        compiler_params=_params(2),
        name="nbr_attention",
    )(rpb.reshape(-1), q, k, v, sz, ck, cv)


def kernel(x_prompt, x_sample, cache_k, cache_v, c, c_ctx, w_ada, b_ada, g_pre, g_post,
           w_in_a, conv_w_a, conv_b_a, ln_g_a, ln_b_a, w_in_b, rpb_b, w_in_c, conv_w_c, w_out):
    depth = w_ada.shape[0]
    d = x_prompt.shape[-1]
    tm = 512
    tm_pre = 1024
    mods = _modulation(c, c_ctx, w_ada, b_ada)
    w_out_bf = w_out.astype(BF16)

    groups = []
    for x, is_ctx in ((x_prompt, True), (x_sample, False)):
        nb, seq, _ = x.shape
        if is_ctx:
            row_fn = lambda tile: (lambda i: 0)
        else:
            row_fn = functools.partial(lambda tile, per: (lambda i: 1 + i // (per // tile)), per=seq)
        x2 = x.reshape(nb * seq, d)
        groups.append(dict(x=x2, h=_prenorm(x2, mods, g_pre, 0, row_fn(tm_pre), tm_pre),
                           nb=nb, seq=seq, row_fn=row_fn(tm), is_ctx=is_ctx))

    new_k, new_v = [], []
    for i in range(depth):
        kind, j = i % N_MIXERS, i // N_MIXERS
        for grp in groups:
            nb, seq = grp["nb"], grp["seq"]
            if kind == 0:
                u, sz = _inproj(grp["h"], w_in_a, j, 0)
                g = _conv_mixer(u, sz, nb, seq, conv_w_a[j], True, conv_b_a[j], ln_g_a[j], ln_b_a[j])
            elif kind == 1:
                if grp["is_ctx"]:
                    q, k, v, sz, k32, v32 = _inproj(grp["h"], w_in_b, j, 1, emit_kv=True)
                    g = _ctx_attention(q, k, v, sz, nb, seq)
                    new_k.append(k32.reshape(nb, seq, -1, HEAD_DIM))
                    new_v.append(v32.reshape(nb, seq, -1, HEAD_DIM))
                else:
                    q, k, v, sz = _inproj(grp["h"], w_in_b, j, 1)
                    g = _nbr_attention(q, k, v, sz, cache_k, cache_v, j, rpb_b[j])
            else:
                cv, bz = _inproj(grp["h"], w_in_c, j, 2)
                g = _conv_mixer(cv, bz, nb, seq, conv_w_c[j], False)
            grp["x"], grp["h"] = _outproj(g, w_out_bf, i, grp["x"], mods, g_post, g_pre, grp["row_fn"], tm)

    y_prompt = groups[0]["x"].reshape(x_prompt.shape)
    y_sample = groups[1]["x"].reshape(x_sample.shape)
    return (y_prompt, y_sample, jnp.stack(new_k, axis=1), jnp.stack(new_v, axis=1))
```

```python
import functools

import numpy as np
import jax
import jax.numpy as jnp
from jax import lax
from jax.experimental import pallas as pl
from jax.experimental.pallas import tpu as pltpu

F32 = jnp.float32
BF16 = jnp.bfloat16

GRID_W = 64
HEAD_DIM = 128
WIN_H = 8
WIN_W = 16
N_MIXERS = 3
EPS = 1e-6
NEG_INF = -1e30
ATTN_SCALE = HEAD_DIM ** -0.5
LOG2_E = 1.4426950408889634

LANES = 128
BF16_ROWS = 16
HALO = BF16_ROWS
CONV_WORDS = 64
MOD_ROWS = 8
VMEM_LIMIT = 56 * 1024 * 1024
IN_SLABS = 4
Q_ROWS = 8
K_ROWS = 16


def _params(n_axes):
    return pltpu.CompilerParams(dimension_semantics=("arbitrary",) * n_axes,
                                vmem_limit_bytes=VMEM_LIMIT)


def _sigmoid(x):
    return 1.0 / (1.0 + jnp.exp2(x * (-LOG2_E)))


def _silu(x):
    return x * _sigmoid(x)


def _mod_kernel(cond_ref, w_ref, b_ref, o_ref):
    s = _silu(cond_ref[...]).astype(BF16)
    o_ref[...] = jnp.dot(s, w_ref[...].astype(BF16), preferred_element_type=F32) + b_ref[...]


def _modulation(c, c_ctx, w_ada, b_ada):
    depth, d, d3 = w_ada.shape
    pad = MOD_ROWS - 1 - c.shape[0]
    cond = jnp.concatenate([c_ctx[None, :], c, jnp.zeros((pad, d), F32)], axis=0)
    tn = 1024
    mods = pl.pallas_call(
        _mod_kernel,
        grid=(depth, d3 // tn),
        in_specs=[pl.BlockSpec((MOD_ROWS, d), lambda l, n: (0, 0)),
                  pl.BlockSpec((None, d, tn), lambda l, n: (l, 0, n)),
                  pl.BlockSpec((None, 1, tn), lambda l, n: (l, 0, n))],
        out_specs=pl.BlockSpec((None, MOD_ROWS, tn), lambda l, n: (l, 0, n)),
        out_shape=jax.ShapeDtypeStruct((depth, MOD_ROWS, d3), F32),
        compiler_params=_params(2),
        name="modulation",
    )(cond, w_ada, b_ada.reshape(depth, 1, d3))
    return mods.reshape(depth * MOD_ROWS, 1, d3)


def _mod_specs(layer, d, row_fn):
    def spec(col):
        return pl.BlockSpec((None, 1, d), lambda i: (layer * MOD_ROWS + row_fn(i), 0, col))
    return spec(0), spec(1), spec(2)


def _adaln(x, g, shift, scale):
    y = x * lax.rsqrt(jnp.mean(x * x, axis=-1, keepdims=True) + EPS)
    return (y * g) * (1.0 + scale) + shift


def _split_rows(tm, cols, index_fn, n_parts=2):
    part = tm // n_parts
    return [pl.BlockSpec((part, cols), functools.partial(
        lambda *idx, p: (index_fn(*idx) * n_parts + p, 0), p=p)) for p in range(n_parts)]


def _prenorm_kernel(n_parts, *refs):
    x_refs = refs[:n_parts]
    g_ref, sh_ref, sc_ref, h_ref = refs[n_parts:]
    rows = x_refs[0].shape[0]
    for p, x_ref in enumerate(x_refs):
        h_ref[p * rows:(p + 1) * rows, :] = _adaln(
            x_ref[...], g_ref[...], sh_ref[...], sc_ref[...]).astype(BF16)


def _prenorm(x, mods, g_pre, layer, row_fn, tm):
    t, d = x.shape
    sh, sc, _ = _mod_specs(layer, d, row_fn)
    return pl.pallas_call(
        functools.partial(_prenorm_kernel, IN_SLABS),
        grid=(t // tm,),
        in_specs=_split_rows(tm, d, lambda i: i, IN_SLABS) + [
            pl.BlockSpec((None, 1, d), lambda i: (layer, 0, 0)), sh, sc],
        out_specs=pl.BlockSpec((tm, d), lambda i: (i, 0)),
        out_shape=jax.ShapeDtypeStruct((t, d), BF16),
        compiler_params=_params(1),
        name="prenorm",
    )(*([x] * IN_SLABS), g_pre.reshape(g_pre.shape[0], 1, d), mods, mods)


def _inproj_kernel(kind, n_groups, tn, emit_kv, n_parts, *refs):
    h_refs = refs[:n_parts]
    refs = refs[n_parts:]
    w_refs = refs[:n_groups]
    out_refs = refs[n_groups:-1]
    wbf_ref = refs[-1]

    @pl.when(pl.program_id(1) == 0)
    def _cast_weights():
        for g in range(n_groups):
            wbf_ref[:, g * tn:(g + 1) * tn] = w_refs[g][...].astype(BF16)

    slab = h_refs[0].shape[0]
    for part, h_ref in enumerate(h_refs):
        rows = slice(part * slab, (part + 1) * slab)
        r = jnp.dot(h_ref[...], wbf_ref[...], preferred_element_type=F32)
        p = [r[:, g * tn:(g + 1) * tn] for g in range(n_groups)]
        if kind == 0:
            a, b, z = p
            out_refs[0][rows, :] = (a * _sigmoid(b)).astype(BF16)
            out_refs[1][rows, :] = _silu(z).astype(BF16)
        elif kind == 1:
            q, k, v, z = p
            out_refs[0][rows, :] = (q * ATTN_SCALE).astype(BF16)
            out_refs[1][rows, :] = k.astype(BF16)
            out_refs[2][rows, :] = v.astype(BF16)
            out_refs[3][rows, :] = _silu(z).astype(BF16)
            if emit_kv:
                out_refs[4][rows, :] = k
                out_refs[5][rows, :] = v
        else:
            bg, cg, v, z = p
            out_refs[0][rows, :] = (cg * v).astype(BF16)
            out_refs[1][rows, :] = (bg * _silu(z)).astype(BF16)


def _inproj(h, w_in, j, kind, emit_kv=False):
    t, d = h.shape
    n_groups = 3 if kind == 0 else 4
    e = w_in.shape[-1] // n_groups
    wide = kind == 0
    tm = min(1024 if (emit_kv or wide) else 2048, t)
    tn = 512 if wide else 256
    nj = e // tn
    w_specs = [pl.BlockSpec((None, d, tn), functools.partial(lambda jj, ii, g: (j, 0, g * nj + jj), g=g))
               for g in range(n_groups)]
    n_bf = {0: 2, 1: 4, 2: 2}[kind]
    out_shape = [jax.ShapeDtypeStruct((t, e), BF16)] * n_bf
    if emit_kv:
        out_shape = out_shape + [jax.ShapeDtypeStruct((t, e), F32)] * 2
    out_spec = pl.BlockSpec((tm, tn), lambda jj, ii: (ii, jj))
    return pl.pallas_call(
        functools.partial(_inproj_kernel, kind, n_groups, tn, emit_kv, IN_SLABS),
        grid=(nj, t // tm),
        in_specs=_split_rows(tm, d, lambda jj, ii: ii, IN_SLABS) + w_specs,
        out_specs=[out_spec] * len(out_shape),
        out_shape=out_shape,
        scratch_shapes=[pltpu.VMEM((d, n_groups * tn), BF16)],
        compiler_params=_params(2),
        name="inproj",
    )(*([h] * IN_SLABS), *([w_in] * n_groups))


def _outproj_kernel(has_next, g0_ref, g1_ref, w_ref, x0_ref, x1_ref, gpost_ref, gate_ref, *refs):
    if has_next:
        gpre_ref, sh_ref, sc_ref, xo_ref, h_ref = refs
    else:
        (xo_ref,) = refs
    half = g0_ref.shape[0]
    for part, (g_ref, x_ref) in enumerate(((g0_ref, x0_ref), (g1_ref, x1_ref))):
        rows = slice(part * half, (part + 1) * half)
        m = jnp.dot(g_ref[...], w_ref[...], preferred_element_type=F32)
        y = m * lax.rsqrt(jnp.mean(m * m, axis=-1, keepdims=True) + EPS)
        xn = x_ref[...] + gate_ref[...] * (y * gpost_ref[...])
        if has_next:
            h_ref[rows, :] = _adaln(xn, gpre_ref[...], sh_ref[...], sc_ref[...]).astype(BF16)
        xo_ref[rows, :] = xn


def _outproj(g, w_out_bf, layer, x, mods, g_post, g_pre, row_fn, tm):
    t, e = g.shape
    d = x.shape[1]
    depth = w_out_bf.shape[0]
    has_next = layer + 1 < depth
    _, _, gate = _mod_specs(layer, d, row_fn)
    row = lambda i: (i, 0)
    in_specs = (_split_rows(tm, e, lambda i: i)
                + [pl.BlockSpec((None, e, d), lambda i: (layer, 0, 0), pipeline_mode=pl.Buffered(1))]
                + _split_rows(tm, d, lambda i: i)
                + [pl.BlockSpec((None, 1, d), lambda i: (layer, 0, 0)), gate])
    args = [g, g, w_out_bf, x, x, g_post.reshape(depth, 1, d), mods]
    out_specs = [pl.BlockSpec((tm, d), row)]
    out_shape = [jax.ShapeDtypeStruct((t, d), F32)]
    if has_next:
        sh, sc, _ = _mod_specs(layer + 1, d, row_fn)
        in_specs += [pl.BlockSpec((None, 1, d), lambda i: (layer + 1, 0, 0)), sh, sc]
        args += [g_pre.reshape(depth, 1, d), mods, mods]
        out_specs.append(pl.BlockSpec((tm, d), row))
        out_shape.append(jax.ShapeDtypeStruct((t, d), BF16))
    outs = pl.pallas_call(
        functools.partial(_outproj_kernel, has_next),
        grid=(t // tm,),
        in_specs=in_specs,
        out_specs=out_specs,
        out_shape=out_shape,
        compiler_params=_params(1),
        name="outproj",
    )(*args)
    return (outs[0], outs[1]) if has_next else (outs[0], None)


def _conv_kernel(taps, conformer, ts, n_col, u_ref, up_ref, un_ref, aux_ref, w_ref, *refs):
    if conformer:
        cb_ref, lg_ref, lb_ref, o_ref, wpk, wb, cbuf = refs
    else:
        o_ref, wpk, wb, cbuf = refs
    s = pl.program_id(1)
    has_prev = s > 0
    has_next = s < pl.num_programs(1) - 1
    pad = taps // 2
    half = ts // 2

    @pl.when((pl.program_id(0) == 0) & (s == 0))
    def _broadcast_weights():
        def body(c, carry):
            for k in range(taps):
                wb[c, k] = jnp.broadcast_to(w_ref[c, k:k + 1, :], (BF16_ROWS, LANES)).astype(BF16)
            return carry
        lax.fori_loop(0, n_col, body, 0)

    def pack(lo, hi):
        return pltpu.pack_elementwise([lo.astype(F32), hi.astype(F32)], packed_dtype=BF16)

    for c in range(n_col):
        sl = slice(c * LANES, (c + 1) * LANES)
        prev = jnp.where(has_prev, up_ref[:, sl].astype(F32), 0.0)
        nxt = jnp.where(has_next, un_ref[:, sl].astype(F32), 0.0)
        wpk[c, 0:HALO, :] = pack(prev, u_ref[half - HALO:half, sl])
        wpk[c, HALO:HALO + half, :] = pack(u_ref[0:half, sl], u_ref[half:ts, sl])
        wpk[c, HALO + half:2 * HALO + half, :] = pack(u_ref[half:half + HALO, sl], nxt)

    def col_body(c, carry):
        def chunk_body(j, carry2):
            w0 = pl.multiple_of(j * CONV_WORDS, CONV_WORDS)
            acc = None
            for k in range(taps):
                x = pltpu.bitcast(wpk[c, pl.ds(w0 + (HALO - pad + k), CONV_WORDS), :], BF16)
                wv = jnp.concatenate([wb[c, k]] * (2 * CONV_WORDS // BF16_ROWS), axis=0)
                term = x.astype(F32) * wv.astype(F32)
                acc = term if acc is None else acc + term
            if conformer:
                acc = acc + cb_ref[c]
            cbuf[c, pl.ds(pl.multiple_of(2 * w0, 2 * CONV_WORDS), 2 * CONV_WORDS), :] = acc
            return carry2
        lax.fori_loop(0, half // CONV_WORDS, chunk_body, 0)
        return carry

    lax.fori_loop(0, n_col, col_body, 0)

    if conformer:
        e = n_col * LANES
        tot = cbuf[0]
        for c in range(1, n_col):
            tot = tot + cbuf[c]
        mu = jnp.sum(tot, axis=-1, keepdims=True) / e
        d0 = cbuf[0] - mu
        tot = d0 * d0
        for c in range(1, n_col):
            dc = cbuf[c] - mu
            tot = tot + dc * dc
        rstd = lax.rsqrt(jnp.sum(tot, axis=-1, keepdims=True) / e + EPS)
        for c in range(n_col):
            cbuf[c] = _silu((cbuf[c] - mu) * rstd * lg_ref[c] + lb_ref[c])
    for c in range(n_col):
        sl = slice(c * LANES, (c + 1) * LANES)
        for i in range(2):
            y = cbuf[c, pl.ds(i, half, stride=2), :]
            rows = slice(i * half, (i + 1) * half)
            o_ref[rows, sl] = (y * aux_ref[rows, sl].astype(F32)).astype(BF16)


def _cols(v, n_col):
    lead = v.shape[:-1]
    v = v.reshape(lead + (n_col, LANES))
    return jnp.moveaxis(v, -2, 0)


def _conv_mixer(u, aux, nb, seq, conv_w, conformer, conv_b=None, ln_g=None, ln_b=None):
    t, e = u.shape
    taps = conv_w.shape[0]
    n_col = e // LANES
    ts = 256
    ns = seq // ts
    hb = ts // HALO
    n_halo = seq // HALO
    u3 = u.reshape(nb, seq, e)
    aux3 = aux.reshape(nb, seq, e)
    tile = pl.BlockSpec((None, ts, e), lambda b, s: (b, s, 0))
    prev = pl.BlockSpec((None, HALO, e), lambda b, s: (b, jnp.maximum(s * hb - 1, 0), 0))
    nxt = pl.BlockSpec((None, HALO, e), lambda b, s: (b, jnp.minimum((s + 1) * hb, n_halo - 1), 0))
    in_specs = [tile, prev, nxt, tile,
                pl.BlockSpec((n_col, taps, LANES), lambda b, s: (0, 0, 0))]
    args = [u3, u3, u3, aux3, _cols(conv_w, n_col)]
    if conformer:
        vec = pl.BlockSpec((n_col, 1, LANES), lambda b, s: (0, 0, 0))
        in_specs += [vec, vec, vec]
        args += [_cols(conv_b[None, :], n_col), _cols(ln_g[None, :], n_col), _cols(ln_b[None, :], n_col)]
    scratch = [pltpu.VMEM((n_col, ts // 2 + 2 * HALO, LANES), jnp.uint32),
               pltpu.VMEM((n_col, taps, BF16_ROWS, LANES), BF16),
               pltpu.VMEM((n_col, ts, LANES), F32)]
    out = pl.pallas_call(
        functools.partial(_conv_kernel, taps, conformer, ts, n_col),
        grid=(nb, ns),
        in_specs=in_specs,
        out_specs=tile,
        out_shape=jax.ShapeDtypeStruct((nb, seq, e), BF16),
        scratch_shapes=scratch,
        compiler_params=_params(2),
        name="conformer_conv" if conformer else "short_conv",
    )(*args)
    return out.reshape(t, e)


def _nt_dot(a, b):
    return lax.dot_general(a, b, (((1,), (1,)), ((), ())), preferred_element_type=F32)


def _ctx_attn_kernel(n_heads, q_ref, k_ref, v_ref, sz_ref, o_ref):
    for hh in range(n_heads):
        sl = slice(hh * HEAD_DIM, (hh + 1) * HEAD_DIM)
        s = _nt_dot(q_ref[:, sl], k_ref[:, sl])
        p = jnp.exp(s - jnp.max(s, axis=-1, keepdims=True))
        l = jnp.sum(p, axis=-1, keepdims=True)
        o = jnp.dot(p.astype(BF16), v_ref[:, sl], preferred_element_type=F32) / l
        o_ref[:, sl] = (o * sz_ref[:, sl].astype(F32)).astype(BF16)


def _ctx_attention(q, k, v, sz, nb, seq):
    t, e = q.shape
    heads_per_step = 16
    w = heads_per_step * HEAD_DIM
    blk = pl.BlockSpec((seq, w), lambda b, g: (b, g))
    return pl.pallas_call(
        functools.partial(_ctx_attn_kernel, heads_per_step),
        grid=(nb, e // w),
        in_specs=[blk] * 4,
        out_specs=blk,
        out_shape=jax.ShapeDtypeStruct((t, e), BF16),
        compiler_params=_params(2),
        name="ctx_attention",
    )(q, k, v, sz)


def _nbr_geometry(rows):
    n_blocks = rows // Q_ROWS
    classes = {}
    for name, m in (("first", 0), ("mid", 1), ("last", n_blocks - 1)):
        kr0 = int(np.clip(Q_ROWS * m - WIN_H // 2, 0, rows - K_ROWS))
        per_row = []
        for i in range(Q_ROWS):
            r = Q_ROWS * m + i
            r0 = int(np.clip(r - WIN_H // 2, 0, rows - WIN_H))
            a = r0 - kr0
            assert 0 <= a and a + WIN_H <= K_ROWS
            jp0, jp1 = a // 2, (a + WIN_H - 1) // 2
            pairs = []
            for jp in range(jp0, jp1 + 1):
                halves = []
                for j in (2 * jp, 2 * jp + 1):
                    ok = a <= j < a + WIN_H
                    halves.append(kr0 + j - r + WIN_H - 1 if ok else None)
                pairs.append(tuple(halves))
            per_row.append((jp0, pairs))
        classes[name] = (kr0 - Q_ROWS * m, per_row)
    return classes


def _nbr_kernel(geom, tab_index, n_bias_rows, n_bias_cols,
                rpb_ref, q_ref, k_ref, v_ref, sz_ref, kc_ref, vc_ref, o_ref,
                half_l, half_r, tabs, s_loc, s_ctx, p_loc, p_ctx, linv, kcb, vcb):
    h = pl.program_id(0)
    b = pl.program_id(1)
    n_blocks = q_ref.shape[0] // (Q_ROWS * GRID_W)
    nq = Q_ROWS * GRID_W
    nk = K_ROWS * GRID_W

    @pl.when(b == 0)
    def _build_bias_tables():
        qc = lax.broadcasted_iota(jnp.int32, (GRID_W, 2 * GRID_W), 0)
        lane = lax.broadcasted_iota(jnp.int32, (GRID_W, 2 * GRID_W), 1)
        kc = lane & (GRID_W - 1)
        right = lane >= GRID_W
        cstart = jnp.clip(qc - WIN_W // 2, 0, GRID_W - WIN_W)
        col_ok = (kc >= cstart) & (kc < cstart + WIN_W)
        dcol = kc - qc + (WIN_W - 1)
        code_l = jnp.where(col_ok & jnp.logical_not(right), dcol, -1)
        code_r = jnp.where(col_ok & right, dcol, -1)
        base = h * (n_bias_rows * n_bias_cols)
        for d in range(n_bias_rows):
            acc_l = jnp.full((GRID_W, 2 * GRID_W), NEG_INF, F32)
            acc_r = acc_l
            for dc in range(n_bias_cols):
                val = rpb_ref[base + d * n_bias_cols + dc]
                acc_l = jnp.where(code_l == dc, val, acc_l)
                acc_r = jnp.where(code_r == dc, val, acc_r)
            half_l[d] = acc_l
            half_r[d] = acc_r
        for (dl, dr), t in tab_index.items():
            if dl is not None and dr is not None:
                tabs[t] = jnp.maximum(half_l[dl], half_r[dr])
            elif dl is not None:
                tabs[t] = half_l[dl]
            else:
                tabs[t] = half_r[dr]

    kcb[...] = kc_ref[...].astype(BF16)
    vcb[...] = vc_ref[...].astype(BF16)

    n_ctx = kcb.shape[0] // LANES

    def geometry(m):
        cls = "first" if m == 0 else ("last" if m == n_blocks - 1 else "mid")
        k_off, per_row = geom[cls]
        return m * nq, m * nq + k_off * GRID_W, per_row

    def scores(m):
        q0, k0, _ = geometry(m)
        q = q_ref[q0:q0 + nq, :]
        s_loc[m % 2] = _nt_dot(q, k_ref[k0:k0 + nk, :])
        s_ctx[m % 2] = _nt_dot(q, kcb[...])

    def softmax(m):
        _, _, per_row = geometry(m)
        sl_ref, sc_ref, pl_ref, pc_ref = s_loc.at[m % 2], s_ctx.at[m % 2], p_loc.at[m % 2], p_ctx.at[m % 2]
        for i in range(Q_ROWS):
            rs = slice(i * GRID_W, (i + 1) * GRID_W)
            jp0, pairs = per_row[i]
            lo, hi = jp0 * LANES, (jp0 + len(pairs)) * LANES

            def local(t):
                cs = slice(lo + t * LANES, lo + (t + 1) * LANES)
                return cs, sl_ref[rs, cs] + tabs[tab_index[pairs[t]]]

            def context(t):
                cs = slice(t * LANES, (t + 1) * LANES)
                return cs, sc_ref[rs, cs]

            vmax = context(0)[1]
            for t in range(1, n_ctx):
                vmax = jnp.maximum(vmax, context(t)[1])
            for t in range(len(pairs)):
                vmax = jnp.maximum(vmax, local(t)[1])
            mx = jnp.max(vmax, axis=-1, keepdims=True)
            vsum = jnp.zeros((GRID_W, LANES), F32)
            for t in range(n_ctx):
                cs, x = context(t)
                px = jnp.exp(x - mx)
                vsum = vsum + px
                pc_ref[rs, cs] = px.astype(BF16)
            for t in range(len(pairs)):
                cs, x = local(t)
                px = jnp.exp(x - mx)
                vsum = vsum + px
                pl_ref[rs, cs] = px.astype(BF16)
            if lo > 0:
                pl_ref[rs, 0:lo] = jnp.zeros((GRID_W, lo), BF16)
            if hi < nk:
                pl_ref[rs, hi:nk] = jnp.zeros((GRID_W, nk - hi), BF16)
            l = jnp.sum(vsum, axis=-1, keepdims=True)
            linv[m % 2, rs, :] = jnp.broadcast_to(1.0 / l, (GRID_W, HEAD_DIM))

    def values(m):
        q0, k0, _ = geometry(m)
        o = jnp.dot(p_loc[m % 2], v_ref[k0:k0 + nk, :], preferred_element_type=F32)
        o = o + jnp.dot(p_ctx[m % 2], vcb[...], preferred_element_type=F32)
        o = o * linv[m % 2] * sz_ref[q0:q0 + nq, :].astype(F32)
        o_ref[q0:q0 + nq, :] = o.astype(BF16)

    scores(0)
    for m in range(n_blocks):
        if m + 1 < n_blocks:
            scores(m + 1)
        softmax(m)
        values(m)


def _nbr_attention(q, k, v, sz, cache_k, cache_v, j, rpb):
    t, e = q.shape
    nb, _, past, n_heads, hd = cache_k.shape
    seq = t // nb
    rows = seq // GRID_W
    assert hd == HEAD_DIM and e == n_heads * HEAD_DIM and seq == rows * GRID_W
    assert rows % Q_ROWS == 0 and rows >= 3 * Q_ROWS and GRID_W * 2 == LANES
    n_bias_rows, n_bias_cols = rpb.shape[1], rpb.shape[2]
    assert n_bias_rows == 2 * WIN_H - 1 and n_bias_cols == 2 * WIN_W - 1
    geom = _nbr_geometry(rows)
    keys = sorted({pr for _, per_row in geom.values() for _, pairs in per_row for pr in pairs},
                  key=lambda pr: (pr[0] is None, pr[1] is None, pr))
    tab_index = {pr: n for n, pr in enumerate(keys)}
    nq, nk = Q_ROWS * GRID_W, K_ROWS * GRID_W
    head = pl.BlockSpec((seq, HEAD_DIM), lambda h, b: (b, h))
    ctx = pl.BlockSpec((None, None, past, HEAD_DIM), lambda h, b: (b, j, 0, h))
    ck = cache_k.reshape(nb, cache_k.shape[1], past, e)
    cv = cache_v.reshape(nb, cache_v.shape[1], past, e)
    return pl.pallas_call(
        functools.partial(_nbr_kernel, geom, tab_index, n_bias_rows, n_bias_cols),
        grid=(n_heads, nb),
        in_specs=[pl.BlockSpec(memory_space=pltpu.SMEM), head, head, head, head, ctx, ctx],
        out_specs=head,
        out_shape=jax.ShapeDtypeStruct((t, e), BF16),
        scratch_shapes=[pltpu.VMEM((n_bias_rows, GRID_W, LANES), F32),
                        pltpu.VMEM((n_bias_rows, GRID_W, LANES), F32),
                        pltpu.VMEM((len(keys), GRID_W, LANES), F32),
                        pltpu.VMEM((2, nq, nk), F32),
                        pltpu.VMEM((2, nq, past), F32),
                        pltpu.VMEM((2, nq, nk), BF16),
                        pltpu.VMEM((2, nq, past), BF16),
                        pltpu.VMEM((2, nq, HEAD_DIM), F32),
                        pltpu.VMEM((past, HEAD_DIM), BF16),
                        pltpu.VMEM((past, HEAD_DIM), BF16)],
        compiler_params=_params(2),
        name="nbr_attention",
    )(rpb.reshape(-1), q, k, v, sz, ck, cv)


def kernel(x_prompt, x_sample, cache_k, cache_v, c, c_ctx, w_ada, b_ada, g_pre, g_post,
           w_in_a, conv_w_a, conv_b_a, ln_g_a, ln_b_a, w_in_b, rpb_b, w_in_c, conv_w_c, w_out):
    depth = w_ada.shape[0]
    d = x_prompt.shape[-1]
    tm = 512
    tm_pre = 1024
    mods = _modulation(c, c_ctx, w_ada, b_ada)
    w_out_bf = w_out.astype(BF16)

    groups = []
    for x, is_ctx in ((x_prompt, True), (x_sample, False)):
        nb, seq, _ = x.shape
        if is_ctx:
            row_fn = lambda tile: (lambda i: 0)
        else:
            row_fn = functools.partial(lambda tile, per: (lambda i: 1 + i // (per // tile)), per=seq)
        x2 = x.reshape(nb * seq, d)
        groups.append(dict(x=x2, h=_prenorm(x2, mods, g_pre, 0, row_fn(tm_pre), tm_pre),
                           nb=nb, seq=seq, row_fn=row_fn(tm), is_ctx=is_ctx))

    new_k, new_v = [], []
    for i in range(depth):
        kind, j = i % N_MIXERS, i // N_MIXERS
        for grp in groups:
            nb, seq = grp["nb"], grp["seq"]
            if kind == 0:
                u, sz = _inproj(grp["h"], w_in_a, j, 0)
                g = _conv_mixer(u, sz, nb, seq, conv_w_a[j], True, conv_b_a[j], ln_g_a[j], ln_b_a[j])
            elif kind == 1:
                if grp["is_ctx"]:
                    q, k, v, sz, k32, v32 = _inproj(grp["h"], w_in_b, j, 1, emit_kv=True)
                    g = _ctx_attention(q, k, v, sz, nb, seq)
                    new_k.append(k32.reshape(nb, seq, -1, HEAD_DIM))
                    new_v.append(v32.reshape(nb, seq, -1, HEAD_DIM))
                else:
                    q, k, v, sz = _inproj(grp["h"], w_in_b, j, 1)
                    g = _nbr_attention(q, k, v, sz, cache_k, cache_v, j, rpb_b[j])
            else:
                cv, bz = _inproj(grp["h"], w_in_c, j, 2)
                g = _conv_mixer(cv, bz, nb, seq, conv_w_c[j], False)
            grp["x"], grp["h"] = _outproj(g, w_out_bf, i, grp["x"], mods, g_post, g_pre, grp["row_fn"], tm)

    y_prompt = groups[0]["x"].reshape(x_prompt.shape)
    y_sample = groups[1]["x"].reshape(x_sample.shape)
    return (y_prompt, y_sample, jnp.stack(new_k, axis=1), jnp.stack(new_v, axis=1))
```
